```python
import jax, jax.numpy as jnp
from jax import lax
import numpy as np

D_MODEL = 1024
BATCH = 8
SEQ = 8192
DEPTH = 1

HEAD_DIM = 64
A_Q_HEADS = 8
A_KV_HEADS = 2
A_GROUP = A_Q_HEADS // A_KV_HEADS
B_GROUPS = ((128, 1), (512, 4), (2048, 16))
B_HEADS_PER_GROUP = 4
B_HEADS = B_HEADS_PER_GROUP * len(B_GROUPS)
GRID_W = 64
ROPE_THETA = 10000.0
Q_BLOCK = 128
N_EXPERTS = 16
EXPERT_FF = 1024
EC_CAPACITY_FACTOR = 2
NORM_EPS = 1e-6
NEG_INF = -1e30

A_Q_W = A_Q_HEADS * HEAD_DIM
A_KV_W = A_KV_HEADS * HEAD_DIM
B_W = B_HEADS * HEAD_DIM
B_OUT_W = B_HEADS_PER_GROUP * HEAD_DIM
N_IN = A_Q_W + 2 * A_KV_W + 3 * B_W + 2 * D_MODEL

kernel_name = "hybrid_gqa2drope_dilated_alibi_ecmoe"


def rms_norm(x, g):
    xf = x.astype(jnp.float32)
    y = xf * lax.rsqrt(jnp.mean(xf * xf, axis=-1, keepdims=True) + NORM_EPS)
    return (y * g.astype(jnp.float32)).astype(x.dtype)


def axial_rope_tables(seq):
    rows = seq // GRID_W
    row_id = jnp.repeat(jnp.arange(rows, dtype=jnp.float32), GRID_W)
    col_id = jnp.tile(jnp.arange(GRID_W, dtype=jnp.float32), rows)
    half = HEAD_DIM // 2
    inv_freq = 1.0 / (ROPE_THETA ** (jnp.arange(0, half, 2, dtype=jnp.float32) / half))
    ang_r = row_id[:, None] * inv_freq[None, :]
    ang_c = col_id[:, None] * inv_freq[None, :]
    return (jnp.cos(ang_r)[:, None, :], jnp.sin(ang_r)[:, None, :],
            jnp.cos(ang_c)[:, None, :], jnp.sin(ang_c)[:, None, :])


def _rotate(x, cos, sin):
    n = x.shape[-1] // 2
    x1, x2 = x[..., :n], x[..., n:]
    return jnp.concatenate([x1 * cos - x2 * sin, x2 * cos + x1 * sin], axis=-1)


def apply_axial_rope(x, tables):
    cos_r, sin_r, cos_c, sin_c = tables
    xf = x.astype(jnp.float32)
    half = HEAD_DIM // 2
    out = jnp.concatenate([_rotate(xf[..., :half], cos_r, sin_r),
                           _rotate(xf[..., half:], cos_c, sin_c)], axis=-1)
    return out.astype(x.dtype)


def axial_gqa_attention(q, k, v, q_g, k_g):
    bsz, seq, _ = q.shape
    q = rms_norm(q.reshape(bsz, seq, A_Q_HEADS, HEAD_DIM), q_g)
    k = rms_norm(k.reshape(bsz, seq, A_KV_HEADS, HEAD_DIM), k_g)
    v = v.reshape(bsz, seq, A_KV_HEADS, HEAD_DIM)
    tables = axial_rope_tables(seq)
    q = apply_axial_rope(q, tables) * (HEAD_DIM ** -0.5)
    k = apply_axial_rope(k, tables)
    q = q.reshape(bsz, seq, A_KV_HEADS, A_GROUP, HEAD_DIM)

    def block(s0):
        qb = lax.dynamic_slice_in_dim(q, s0, Q_BLOCK, axis=1)
        s = jnp.einsum('bqkgd,bskd->bkgqs', qb, k).astype(jnp.float32)
        p = jax.nn.softmax(s, axis=-1).astype(v.dtype)
        return jnp.einsum('bkgqs,bskd->bqkgd', p, v)

    starts = jnp.arange(seq // Q_BLOCK, dtype=jnp.int32) * Q_BLOCK
    o = lax.map(block, starts)
    return jnp.moveaxis(o, 0, 1).reshape(bsz, seq, A_Q_W)


def dilated_window_attention(q, k, v):
    bsz, seq, _ = q.shape
    n_g = len(B_GROUPS)
    q = q.reshape(bsz, seq, n_g, B_HEADS_PER_GROUP, HEAD_DIM) * (HEAD_DIM ** -0.5)
    k = k.reshape(bsz, seq, n_g, B_HEADS_PER_GROUP, HEAD_DIM)
    v = v.reshape(bsz, seq, n_g, B_HEADS_PER_GROUP, HEAD_DIM)
    qs = [q[:, :, gi] for gi in range(n_g)]
    ks = [k[:, :, gi] for gi in range(n_g)]
    vs = [v[:, :, gi] for gi in range(n_g)]
    slopes = jnp.exp2(-8.0 * jnp.arange(1, B_HEADS + 1, dtype=jnp.float32) / B_HEADS)
    slopes = slopes.reshape(n_g, B_HEADS_PER_GROUP)

    def block(s0):
        t = s0 + jnp.arange(Q_BLOCK, dtype=jnp.int32)
        outs, lses = [], []
        for gi, (window, dil) in enumerate(B_GROUPS):
            n_side = window // (2 * dil)
            off = np.arange(-n_side, n_side + 1, dtype=np.int32) * dil
            idx = t[:, None] + jnp.asarray(off)[None, :]
            valid = (idx >= 0) & (idx < seq)
            idxc = jnp.clip(idx, 0, seq - 1)
            qb = lax.dynamic_slice_in_dim(qs[gi], s0, Q_BLOCK, axis=1)
            kg = ks[gi][:, idxc]
            vg = vs[gi][:, idxc]
            s = jnp.einsum('bqhd,bqkhd->bhqk', qb, kg).astype(jnp.float32)
            s = s - slopes[gi][:, None, None] * jnp.abs(jnp.asarray(off, jnp.float32))[None, None, :]
            s = jnp.where(valid[None, None], s, NEG_INF)
            lse = jax.nn.logsumexp(s, axis=-1)
            p = jnp.exp(s - lse[..., None]).astype(vg.dtype)
            outs.append(jnp.einsum('bhqk,bqkhd->bqhd', p, vg))
            lses.append(lse)
        w = jax.nn.softmax(jnp.stack(lses, axis=0), axis=0)
        w = jnp.swapaxes(w, 2, 3)[..., None]
        o = jnp.sum(jnp.stack(outs, axis=0).astype(jnp.float32) * w, axis=0)
        return o.astype(v.dtype)

    starts = jnp.arange(seq // Q_BLOCK, dtype=jnp.int32) * Q_BLOCK
    o = lax.map(block, starts)
    return jnp.moveaxis(o, 0, 1).reshape(bsz, seq, B_OUT_W)


def expert_choice_swiglu(h, w_router, w_gate, w_up, w_down):
    bsz, seq, dm = h.shape
    cap = EC_CAPACITY_FACTOR * seq // N_EXPERTS
    aff = jax.nn.softmax(jnp.einsum('bsd,de->bse', h, w_router).astype(jnp.float32), axis=-1)
    g, idx = lax.top_k(jnp.swapaxes(aff, 1, 2), cap)
    xin = jax.vmap(lambda hb, ib: hb[ib])(h, idx)
    a = jnp.einsum('becd,edf->becf', xin, w_gate)
    u = jnp.einsum('becd,edf->becf', xin, w_up)
    y = jnp.einsum('becf,efd->becd', jax.nn.silu(a) * u, w_down) * g[..., None].astype(h.dtype)
    return jax.vmap(lambda ib, yb: jnp.zeros((seq, dm), yb.dtype).at[ib.reshape(-1)].add(yb.reshape(-1, dm)))(idx, y)


def setup_inputs(seed: int = 0) -> dict:
    key = jax.random.key(seed)
    ks = jax.random.split(key, 16)
    f32 = jnp.float32
    nrm = lambda k, shape, fan_in: jax.random.normal(k, shape, f32) * (fan_in ** -0.5)
    gain = lambda k, shape: 1.0 + 0.02 * jax.random.normal(k, shape, f32)
    return {
        "x": jax.random.normal(ks[0], (BATCH, SEQ, D_MODEL), f32),
        "norm1_g": gain(ks[1], (DEPTH, D_MODEL)),
        "w_in": nrm(ks[2], (DEPTH, D_MODEL, N_IN), D_MODEL),
        "b_gates": 0.1 * jax.random.normal(ks[3], (DEPTH, 2 * D_MODEL), f32),
        "q_norm_g": gain(ks[4], (DEPTH, HEAD_DIM)),
        "k_norm_g": gain(ks[5], (DEPTH, HEAD_DIM)),
        "w_o_a": nrm(ks[6], (DEPTH, A_Q_W, D_MODEL), A_Q_W),
        "w_o_b": nrm(ks[7], (DEPTH, B_OUT_W, D_MODEL), B_OUT_W),
        "w_out": nrm(ks[8], (DEPTH, D_MODEL, D_MODEL), D_MODEL),
        "norm2_g": gain(ks[9], (DEPTH, D_MODEL)),
        "w_router": nrm(ks[10], (DEPTH, D_MODEL, N_EXPERTS), D_MODEL),
        "w_gate": nrm(ks[11], (DEPTH, N_EXPERTS, D_MODEL, EXPERT_FF), D_MODEL),
        "w_up": nrm(ks[12], (DEPTH, N_EXPERTS, D_MODEL, EXPERT_FF), D_MODEL),
        "w_down": nrm(ks[13], (DEPTH, N_EXPERTS, EXPERT_FF, D_MODEL), EXPERT_FF),
        "normf_g": gain(ks[14], (D_MODEL,)),
    }


def reference(x, norm1_g, w_in, b_gates, q_norm_g, k_norm_g, w_o_a, w_o_b, w_out,
              norm2_g, w_router, w_gate, w_up, w_down, normf_g):
    splits = np.cumsum([A_Q_W, A_KV_W, A_KV_W, B_W, B_W, B_W, D_MODEL])
    for l in range(DEPTH):
        h = rms_norm(x, norm1_g[l])
        z = jnp.einsum('bsd,dn->bsn', h, w_in[l])
        qa, ka, va, qb, kb, vb, ga, gb = jnp.split(z, splits, axis=-1)
        gate_a = jax.nn.sigmoid(ga + b_gates[l, :D_MODEL])
        gate_b = jax.nn.sigmoid(gb + b_gates[l, D_MODEL:])
        ya = jnp.einsum('bsc,cd->bsd', axial_gqa_attention(qa, ka, va, q_norm_g[l], k_norm_g[l]), w_o_a[l])
        yb = jnp.einsum('bsc,cd->bsd', dilated_window_attention(qb, kb, vb), w_o_b[l])
        x = x + jnp.einsum('bsc,cd->bsd', gate_a * ya + gate_b * yb, w_out[l])
        x = x + expert_choice_swiglu(rms_norm(x, norm2_g[l]), w_router[l], w_gate[l], w_up[l], w_down[l])
    return rms_norm(x, normf_g)
```

```python
import functools

import numpy as np
import jax
import jax.numpy as jnp
from jax import lax
from jax.experimental import pallas as pl
from jax.experimental.pallas import tpu as pltpu

HEAD_DIM = 64
A_Q_HEADS = 8
A_KV_HEADS = 2
A_GROUP = A_Q_HEADS // A_KV_HEADS
B_GROUPS = ((128, 1), (512, 4), (2048, 16))
B_HEADS_PER_GROUP = 4
B_HEADS = B_HEADS_PER_GROUP * len(B_GROUPS)
GRID_W = 64
ROPE_THETA = 10000.0
N_EXPERTS = 16
EC_CAPACITY_FACTOR = 2
NORM_EPS = 1e-6
NEG_INF = -1e30

A_Q_W = A_Q_HEADS * HEAD_DIM
A_KV_W = A_KV_HEADS * HEAD_DIM
B_W = B_HEADS * HEAD_DIM
B_OUT_W = B_HEADS_PER_GROUP * HEAD_DIM

LANES = 128
MXU_DIM = 256
VMEM_LIMIT = 56 * 1024 * 1024

F32 = jnp.float32
BF16 = jnp.bfloat16


def _nt_dot(a, b):
    return lax.dot_general(a, b, (((1,), (1,)), ((), ())), preferred_element_type=F32)


def _dot(a, b):
    return jnp.dot(a, b, preferred_element_type=F32)


def _split_bf16(v):
    hi = v.astype(BF16)
    lo = (v - hi.astype(F32)).astype(BF16)
    return hi, lo


def _in_proj_body(x_ref, g1_ref, w_ref, bg_ref, qg_ref, kg_ref, cos_ref, sin_ref, hs_ref,
                  qa_ref, ka_ref, va_ref, qb_ref, kb_ref, vb_ref, ga_ref, gb_ref, *, d_model):
    x = x_ref[0]
    ms = jnp.mean(x * x, axis=-1, keepdims=True)
    h = (x * lax.rsqrt(ms + NORM_EPS) * g1_ref[...]).astype(BF16)

    def proj(lo, width):
        return _dot(h, w_ref[:, lo:lo + width])

    cos = cos_ref[...]
    sin = sin_ref[...]
    lane = lax.broadcasted_iota(jnp.int32, (1, LANES), 1)
    low = (lane % 32) < 16

    def head_norm_rope(z, gain, nrep):
        width = LANES * nrep
        z2 = z * z
        hi, lo = _split_bf16(z2)
        hs = hs_ref[:width, :width]
        ss = _dot(hi, hs) + _dot(lo, hs)
        zn = z * lax.rsqrt(ss * (1.0 / HEAD_DIM) + NORM_EPS) * gain
        rep = lambda t: jnp.concatenate([t] * nrep, axis=1) if nrep > 1 else t
        partner = jnp.where(rep(low), pltpu.roll(zn, width - 16, 1), pltpu.roll(zn, 16, 1))
        return zn * rep(cos) + partner * rep(sin)

    scale = HEAD_DIM ** -0.5
    za = proj(0, A_Q_W + 2 * A_KV_W)
    qa = head_norm_rope(za[:, :A_Q_W], qg_ref[...], A_Q_W // LANES) * scale
    ka = head_norm_rope(za[:, A_Q_W:A_Q_W + A_KV_W], kg_ref[...], A_KV_W // LANES)
    qa_ref[0] = qa.astype(BF16)
    ka_ref[0] = ka.astype(BF16)
    va_ref[0] = za[:, A_Q_W + A_KV_W:].astype(BF16)
    off = A_Q_W + 2 * A_KV_W
    qb_ref[0] = (proj(off, B_W) * scale).astype(BF16)
    kb_ref[0] = proj(off + B_W, B_W).astype(BF16)
    vb_ref[0] = proj(off + 2 * B_W, B_W).astype(BF16)
    off += 3 * B_W
    ga_ref[0] = jax.nn.sigmoid(proj(off, d_model) + bg_ref[:, :d_model]).astype(BF16)
    gb_ref[0] = jax.nn.sigmoid(proj(off + d_model, d_model) + bg_ref[:, d_model:]).astype(BF16)


def _rope_tables(seq):
    rows = seq // GRID_W
    row_id = jnp.repeat(jnp.arange(rows, dtype=F32), GRID_W)
    col_id = jnp.tile(jnp.arange(GRID_W, dtype=F32), rows)
    half = HEAD_DIM // 2
    inv_freq = 1.0 / (ROPE_THETA ** (jnp.arange(0, half, 2, dtype=F32) / half))
    ang_r = row_id[:, None] * inv_freq[None, :]
    ang_c = col_id[:, None] * inv_freq[None, :]
    cos = jnp.concatenate([jnp.cos(ang_r)] * 2 + [jnp.cos(ang_c)] * 2, axis=1)
    sin = jnp.concatenate([-jnp.sin(ang_r), jnp.sin(ang_r), -jnp.sin(ang_c), jnp.sin(ang_c)], axis=1)
    return jnp.tile(cos, (1, 2)), jnp.tile(sin, (1, 2))


def _in_proj(x, norm1_g, w_in, b_gates, q_norm_g, k_norm_g, tm):
    bsz, seq, d_model = x.shape
    n_in = w_in.shape[1]
    cos, sin = _rope_tables(seq)
    head_id = np.arange(A_Q_W) // HEAD_DIM
    hs = jnp.asarray(head_id[:, None] == head_id[None, :], BF16)
    qg = jnp.tile(q_norm_g, A_Q_W // HEAD_DIM)[None, :]
    kg = jnp.tile(k_norm_g, A_KV_W // HEAD_DIM)[None, :]
    row = lambda w: pl.BlockSpec((1, tm, w), lambda b, i: (b, i, 0))
    full = lambda a: pl.BlockSpec(a.shape, lambda b, i: (0,) * a.ndim)
    tab = pl.BlockSpec((tm, LANES), lambda b, i: (i, 0))
    g1 = norm1_g[None, :]
    bg = b_gates[None, :]
    w = w_in.astype(BF16)
    shp = lambda wd: jax.ShapeDtypeStruct((bsz, seq, wd), BF16)
    return pl.pallas_call(
        functools.partial(_in_proj_body, d_model=d_model),
        grid=(bsz, seq // tm),
        in_specs=[row(d_model), full(g1), full(w), full(bg), full(qg), full(kg), tab, tab, full(hs)],
        out_specs=[row(A_Q_W), row(A_KV_W), row(A_KV_W), row(B_W), row(B_W), row(B_W),
                   row(d_model), row(d_model)],
        out_shape=[shp(A_Q_W), shp(A_KV_W), shp(A_KV_W), shp(B_W), shp(B_W), shp(B_W),
                   shp(d_model), shp(d_model)],
        compiler_params=pltpu.CompilerParams(
            dimension_semantics=("parallel", "parallel"), vmem_limit_bytes=VMEM_LIMIT),
        name="in_proj",
    )(x, g1, w, bg, qg, kg, cos, sin, hs)


def _attn_a_body(q_ref, k_ref, v_ref, o_ref, *, tk):
    tq = q_ref.shape[1]
    seq = k_ref.shape[1]
    q = q_ref[0].astype(F32)
    zeros = jnp.zeros((tq, HEAD_DIM), F32)
    outs = []
    for g in range(A_Q_HEADS):
        kv = g // A_GROUP
        qg = q[:, g * HEAD_DIM:(g + 1) * HEAD_DIM]
        qp = jnp.concatenate([qg, zeros] if kv == 0 else [zeros, qg], axis=1).astype(BF16)

        def step(kt, carry, qp=qp):
            m, l, acc = carry
            start = pl.multiple_of(kt * tk, tk)
            k = k_ref[0, pl.ds(start, tk), :]
            v = v_ref[0, pl.ds(start, tk), :]
            s = _nt_dot(qp, k)
            m_new = jnp.maximum(m, jnp.max(s, axis=1, keepdims=True))
            alpha = jnp.exp(m - m_new)
            p = jnp.exp(s - m_new)
            l = alpha * l + jnp.sum(p, axis=1, keepdims=True)
            acc = alpha * acc + _dot(p.astype(BF16), v)
            return m_new, l, acc

        init = (jnp.full((tq, 1), NEG_INF, F32), jnp.zeros((tq, 1), F32),
                jnp.zeros((tq, A_KV_W), F32))
        m, l, acc = lax.fori_loop(0, seq // tk, step, init)
        outs.append(acc[:, kv * HEAD_DIM:(kv + 1) * HEAD_DIM] / l)
    o_ref[0] = jnp.concatenate(outs, axis=1).astype(o_ref.dtype)


def _attn_a(qa, ka, va, tq, tk):
    bsz, seq, _ = qa.shape
    return pl.pallas_call(
        functools.partial(_attn_a_body, tk=tk),
        grid=(bsz, seq // tq),
        in_specs=[pl.BlockSpec((1, tq, A_Q_W), lambda b, i: (b, i, 0)),
                  pl.BlockSpec((1, seq, A_KV_W), lambda b, i: (b, 0, 0)),
                  pl.BlockSpec((1, seq, A_KV_W), lambda b, i: (b, 0, 0))],
        out_specs=pl.BlockSpec((1, tq, A_Q_W), lambda b, i: (b, i, 0)),
        out_shape=jax.ShapeDtypeStruct((bsz, seq, A_Q_W), BF16),
        compiler_params=pltpu.CompilerParams(
            dimension_semantics=("parallel", "parallel"), vmem_limit_bytes=VMEM_LIMIT),
        name="attn_a",
    )(qa, ka, va)


HALO = 128
SUB = 128
N_SIDE = 64


def _attn_b_body(q_ref, kp_ref, kc_ref, kn_ref, vp_ref, vc_ref, vn_ref, o_ref, lse_ref,
                 kbuf, vbuf, *, dil, group, seq_d):
    tu = q_ref.shape[1]
    i = pl.program_id(2)
    kbuf[0:HALO] = kp_ref[0]
    kbuf[HALO:HALO + tu] = kc_ref[0]
    kbuf[HALO + tu:] = kn_ref[0]
    vbuf[0:HALO] = vp_ref[0]
    vbuf[HALO:HALO + tu] = vc_ref[0]
    vbuf[HALO + tu:] = vn_ref[0]

    win = SUB + 2 * N_SIDE
    row = lax.broadcasted_iota(jnp.int32, (SUB, win), 0)
    col = lax.broadcasted_iota(jnp.int32, (SUB, win), 1)
    rel = col - N_SIDE - row
    in_band = jnp.abs(rel) <= N_SIDE
    dist = (jnp.abs(rel) * dil).astype(F32)
    head_of_lane = lax.broadcasted_iota(jnp.int32, (1, B_OUT_W), 1) // HEAD_DIM

    for j in range(tu // SUB):
        qs = q_ref[0, j * SUB:(j + 1) * SUB, :]
        w0 = HALO - N_SIDE + j * SUB
        kw = kbuf[w0:w0 + win, :]
        vw = vbuf[w0:w0 + win, :]
        key_u = i * tu + j * SUB - N_SIDE + col
        ok = in_band & (key_u >= 0) & (key_u < seq_d)
        o_acc = jnp.zeros((SUB, B_OUT_W), F32)
        lse_acc = jnp.zeros((SUB, B_OUT_W), F32)
        for hh in range(B_HEADS_PER_GROUP):
            slope = 2.0 ** (-8.0 * (group * B_HEADS_PER_GROUP + hh + 1) / B_HEADS)
            mine = head_of_lane == hh
            qm = jnp.where(mine, qs, jnp.zeros_like(qs))
            s = _nt_dot(qm, kw) - slope * dist
            s = jnp.where(ok, s, NEG_INF)
            m = jnp.max(s, axis=1, keepdims=True)
            p = jnp.exp(s - m)
            l = jnp.sum(p, axis=1, keepdims=True)
            pv = _dot(p.astype(BF16), vw)
            o_acc = jnp.where(mine, pv / l, o_acc)
            lse_acc = jnp.where(mine, m + jnp.log(l), lse_acc)
        o_ref[0, j * SUB:(j + 1) * SUB, :] = o_acc
        lse_ref[0, j * SUB:(j + 1) * SUB, :] = lse_acc


def _attn_b_group(qb, kb, vb, group, dil, tu):
    bsz, seq, _ = qb.shape
    seq_d = seq // dil
    tu = min(tu, seq_d)
    per = tu // HALO
    last = seq_d // HALO - 1
    view = lambda a: a.reshape(bsz, seq_d, dil * B_W)
    ncol = B_W // B_OUT_W
    colb = lambda r: r * ncol + group
    cur = pl.BlockSpec((1, tu, B_OUT_W), lambda b, r, i: (b, i, colb(r)))
    prev = pl.BlockSpec((1, HALO, B_OUT_W), lambda b, r, i: (b, jnp.maximum(i * per - 1, 0), colb(r)))
    nxt = pl.BlockSpec((1, HALO, B_OUT_W), lambda b, r, i: (b, jnp.minimum((i + 1) * per, last), colb(r)))
    out = pl.BlockSpec((1, tu, B_OUT_W), lambda b, r, i: (b, i, r))
    o, lse = pl.pallas_call(
        functools.partial(_attn_b_body, dil=dil, group=group, seq_d=seq_d),
        grid=(bsz, dil, seq_d // tu),
        in_specs=[cur, prev, cur, nxt, prev, cur, nxt],
        out_specs=[out, out],
        out_shape=[jax.ShapeDtypeStruct((bsz, seq_d, dil * B_OUT_W), F32)] * 2,
        scratch_shapes=[pltpu.VMEM((tu + 2 * HALO, B_OUT_W), BF16)] * 2,
        compiler_params=pltpu.CompilerParams(
            dimension_semantics=("parallel", "parallel", "parallel"), vmem_limit_bytes=VMEM_LIMIT),
        name=f"attn_b{group}",
    )(view(qb), view(kb), view(kb), view(kb), view(vb), view(vb), view(vb))
    return o.reshape(bsz, seq, B_OUT_W), lse.reshape(bsz, seq, B_OUT_W)


def _merge_body(x_ref, oa_ref, o0_ref, o1_ref, o2_ref, l0_ref, l1_ref, l2_ref, ga_ref, gb_ref,
                woa_ref, wob_ref, wout_ref, g2_ref, wrh_ref, wrl_ref,
                x2_ref, h2_ref, aff_ref):
    l0, l1, l2 = l0_ref[0], l1_ref[0], l2_ref[0]
    mx = jnp.maximum(jnp.maximum(l0, l1), l2)
    e0, e1, e2 = jnp.exp(l0 - mx), jnp.exp(l1 - mx), jnp.exp(l2 - mx)
    ob = (e0 * o0_ref[0] + e1 * o1_ref[0] + e2 * o2_ref[0]) / (e0 + e1 + e2)
    ya = _dot(oa_ref[0], woa_ref[...])
    yb = _dot(ob.astype(BF16), wob_ref[...])
    mrg = ga_ref[0].astype(F32) * ya + gb_ref[0].astype(F32) * yb
    x2 = x_ref[0] + _dot(mrg.astype(BF16), wout_ref[...])
    x2_ref[0] = x2
    ms = jnp.mean(x2 * x2, axis=-1, keepdims=True)
    h2 = x2 * lax.rsqrt(ms + NORM_EPS) * g2_ref[...]
    h2_ref[0] = h2.astype(BF16)
    hh, hl = _split_bf16(h2)
    wh, wl = wrh_ref[...], wrl_ref[...]
    logits = _nt_dot(wh, hh) + _nt_dot(wh, hl) + _nt_dot(wl, hh)
    mxl = jnp.max(logits, axis=0, keepdims=True)
    ex = jnp.exp(logits - mxl)
    aff_ref[0] = ex / jnp.sum(ex, axis=0, keepdims=True)


def _merge(x, oa, obs, lses, gate_a, gate_b, w_o_a, w_o_b, w_out, norm2_g, w_router, tm):
    bsz, seq, d_model = x.shape
    row = lambda w: pl.BlockSpec((1, tm, w), lambda b, i: (b, i, 0))
    full = lambda a: pl.BlockSpec(a.shape, lambda b, i: (0,) * a.ndim)
    woa, wob, wout = w_o_a.astype(BF16), w_o_b.astype(BF16), w_out.astype(BF16)
    g2 = norm2_g[None, :]
    wrh, wrl = _split_bf16(w_router.T)
    return pl.pallas_call(
        _merge_body,
        grid=(bsz, seq // tm),
        in_specs=[row(d_model), row(A_Q_W)] + [row(B_OUT_W)] * 6 + [row(d_model)] * 2
                 + [full(woa), full(wob), full(wout), full(g2), full(wrh), full(wrl)],
        out_specs=[row(d_model), row(d_model),
                   pl.BlockSpec((1, N_EXPERTS, tm), lambda b, i: (b, 0, i))],
        out_shape=[jax.ShapeDtypeStruct((bsz, seq, d_model), F32),
                   jax.ShapeDtypeStruct((bsz, seq, d_model), BF16),
                   jax.ShapeDtypeStruct((bsz, N_EXPERTS, seq), F32)],
        compiler_params=pltpu.CompilerParams(
            dimension_semantics=("parallel", "parallel"), vmem_limit_bytes=VMEM_LIMIT),
        name="merge",
    )(x, oa, *obs, *lses, gate_a, gate_b, woa, wob, wout, g2, wrh, wrl)


TOK_BLK = 256
GATHER_ROWS = 128
META_LO = 64
META_HI = 96


def _route_body(aff_ref, pos_ref, post_ref, gt_ref, meta_ref, *, cap):
    aff = aff_ref[0]
    n_e, seq = aff.shape
    bits = pltpu.bitcast(aff, jnp.int32)

    def refine(it, thr):
        cand = thr | jnp.left_shift(jnp.int32(1), 30 - it)
        cnt = jnp.sum(jnp.where(bits >= cand, 1.0, 0.0), axis=1, keepdims=True)
        return jnp.where(cnt >= cap, cand, thr)

    thr = lax.fori_loop(0, 31, refine, jnp.zeros((n_e, 1), jnp.int32))
    above = bits > thr
    tied = bits == thr
    need = cap - jnp.sum(jnp.where(above, 1.0, 0.0), axis=1, keepdims=True)

    ri = lax.broadcasted_iota(jnp.int32, (TOK_BLK, TOK_BLK), 0)
    ci = lax.broadcasted_iota(jnp.int32, (TOK_BLK, TOK_BLK), 1)
    before = jnp.where(ri < ci, 1.0, 0.0).astype(BF16)
    lane = lax.broadcasted_iota(jnp.int32, (1, LANES), 1)
    pad_rows = LANES - n_e

    run_tied = jnp.zeros((n_e, 1), F32)
    run_sel = jnp.zeros((n_e, 1), F32)
    starts = jnp.zeros((n_e, LANES), F32)
    ends = jnp.zeros((n_e, LANES), F32)
    n_blk = seq // TOK_BLK
    for c in range(n_blk):
        sl = slice(c * TOK_BLK, (c + 1) * TOK_BLK)
        tied_c = jnp.where(tied[:, sl], 1.0, 0.0)
        rank = run_tied + _dot(tied_c.astype(BF16), before)
        sel = above[:, sl] | (tied[:, sl] & (rank < need))
        sel_c = jnp.where(sel, 1.0, 0.0)
        pos = run_sel + _dot(sel_c.astype(BF16), before)
        pos_m = jnp.where(sel, pos, -1.0)
        g_m = jnp.where(sel, aff[:, sl], 0.0)
        pos_ref[0, :, sl] = pos_m
        fill = jnp.full((pad_rows, TOK_BLK), -1.0, F32)
        post_ref[0, sl, :] = jnp.concatenate([pos_m, fill], axis=0).T
        gt_ref[0, sl, :] = jnp.concatenate([g_m, jnp.zeros((pad_rows, TOK_BLK), F32)], axis=0).T
        starts = jnp.where(lane == c, run_sel, starts)
        run_tied = run_tied + jnp.sum(tied_c, axis=1, keepdims=True)
        run_sel = run_sel + jnp.sum(sel_c, axis=1, keepdims=True)
        ends = jnp.where(lane == c, run_sel, ends)
    meta = jnp.where(lane == n_blk, run_sel, starts)
    real = lane < n_blk
    for j in range(cap // GATHER_ROWS):
        c0 = float(j * GATHER_ROWS)
        first = jnp.sum(jnp.where(real & (ends <= c0), 1.0, 0.0), axis=1, keepdims=True)
        stop = n_blk - jnp.sum(jnp.where(real & (starts >= c0 + GATHER_ROWS), 1.0, 0.0), axis=1, keepdims=True)
        meta = jnp.where(lane == META_LO + j, first, meta)
        meta = jnp.where(lane == META_HI + j, stop, meta)
    meta_ref[0] = meta.astype(jnp.int32)


def _route(aff, cap):
    bsz, n_e, seq = aff.shape
    return pl.pallas_call(
        functools.partial(_route_body, cap=cap),
        grid=(bsz,),
        in_specs=[pl.BlockSpec((1, n_e, seq), lambda b: (b, 0, 0))],
        out_specs=[pl.BlockSpec((1, n_e, seq), lambda b: (b, 0, 0)),
                   pl.BlockSpec((1, seq, LANES), lambda b: (b, 0, 0)),
                   pl.BlockSpec((1, seq, LANES), lambda b: (b, 0, 0)),
                   pl.BlockSpec((1, n_e, LANES), lambda b: (b, 0, 0))],
        out_shape=[jax.ShapeDtypeStruct((bsz, n_e, seq), F32),
                   jax.ShapeDtypeStruct((bsz, seq, LANES), F32),
                   jax.ShapeDtypeStruct((bsz, seq, LANES), F32),
                   jax.ShapeDtypeStruct((bsz, n_e, LANES), jnp.int32)],
        compiler_params=pltpu.CompilerParams(
            dimension_semantics=("parallel",), vmem_limit_bytes=VMEM_LIMIT),
        name="route",
    )(aff)


def _gather_body(meta_ref, h_ref, pos_ref, xin_ref, acc_ref, *, cap):
    b, e = pl.program_id(0), pl.program_id(1)
    base = (b * N_EXPERTS + e) * LANES
    slot = lax.broadcasted_iota(jnp.int32, (GATHER_ROWS, 1), 0)
    for j in range(cap // GATHER_ROWS):
        want = (slot + j * GATHER_ROWS).astype(F32)
        acc_ref[...] = jnp.zeros_like(acc_ref)

        def add_block(k, carry, want=want):
            pos = pos_ref[0, 0, pl.ds(k, 1), :]
            onehot = jnp.where(pos == want, 1.0, 0.0).astype(BF16)
            hblk = h_ref[0, pl.ds(pl.multiple_of(k * TOK_BLK, TOK_BLK), TOK_BLK), :]
            acc_ref[...] += _dot(onehot, hblk)
            return carry

        lax.fori_loop(meta_ref[base + META_LO + j], meta_ref[base + META_HI + j], add_block, 0)
        xin_ref[0, 0, j * GATHER_ROWS:(j + 1) * GATHER_ROWS, :] = acc_ref[...].astype(xin_ref.dtype)


def _gather(meta_flat, h2, pos, cap):
    bsz, seq, d_model = h2.shape
    n_e = pos.shape[1]
    pos4 = pos.reshape(bsz, n_e, seq // TOK_BLK, TOK_BLK)
    return pl.pallas_call(
        functools.partial(_gather_body, cap=cap),
        grid_spec=pltpu.PrefetchScalarGridSpec(
            num_scalar_prefetch=1,
            grid=(bsz, n_e),
            in_specs=[pl.BlockSpec((1, seq, d_model), lambda b, e, m: (b, 0, 0)),
                      pl.BlockSpec((1, 1, seq // TOK_BLK, TOK_BLK), lambda b, e, m: (b, e, 0, 0))],
            out_specs=pl.BlockSpec((1, 1, cap, d_model), lambda b, e, m: (b, e, 0, 0)),
            scratch_shapes=[pltpu.VMEM((GATHER_ROWS, d_model), F32)]),
        out_shape=jax.ShapeDtypeStruct((bsz, n_e, cap, d_model), BF16),
        compiler_params=pltpu.CompilerParams(
            dimension_semantics=("parallel", "parallel"), vmem_limit_bytes=VMEM_LIMIT),
        name="gather",
    )(meta_flat, h2, pos4)


FFN_ROWS = 256


def _ffn_body(x_ref, wg_ref, wu_ref, wd_ref, y_ref):
    cap = x_ref.shape[2]
    for r in range(cap // FFN_ROWS):
        rows = slice(r * FFN_ROWS, (r + 1) * FFN_ROWS)
        xin = x_ref[0, 0, rows, :]
        a = _dot(xin, wg_ref[0])
        u = _dot(xin, wu_ref[0])
        mid = (a * jax.nn.sigmoid(a) * u).astype(BF16)
        y_ref[0, 0, rows, :] = _dot(mid, wd_ref[0]).astype(y_ref.dtype)


def _ffn(xin, w_gate, w_up, w_down):
    bsz, n_e, cap, d_model = xin.shape
    ff = w_gate.shape[2]
    tok = pl.BlockSpec((1, 1, cap, d_model), lambda e, b: (b, e, 0, 0))
    return pl.pallas_call(
        _ffn_body,
        grid=(n_e, bsz),
        in_specs=[tok,
                  pl.BlockSpec((1, d_model, ff), lambda e, b: (e, 0, 0)),
                  pl.BlockSpec((1, d_model, ff), lambda e, b: (e, 0, 0)),
                  pl.BlockSpec((1, ff, d_model), lambda e, b: (e, 0, 0))],
        out_specs=tok,
        out_shape=jax.ShapeDtypeStruct((bsz, n_e, cap, d_model), BF16),
        compiler_params=pltpu.CompilerParams(
            dimension_semantics=("parallel", "parallel"), vmem_limit_bytes=VMEM_LIMIT),
        name="ffn",
    )(xin, w_gate.astype(BF16), w_up.astype(BF16), w_down.astype(BF16))


SCATTER_ROWS = 256


def _scatter_body(meta_ref, y_ref, post_ref, gt_ref, x2_ref, gf_ref, o_ref, *, tc):
    b, t, e = pl.program_id(0), pl.program_id(1), pl.program_id(2)
    base = (b * N_EXPERTS + e) * LANES

    @pl.when(e == 0)
    def _():
        o_ref[0] = x2_ref[0]

    lane = lax.broadcasted_iota(jnp.int32, (1, LANES), 1)
    mine = lane == e
    slot = lax.broadcasted_iota(jnp.int32, (1, SCATTER_ROWS), 1)
    for kk in range(tc // TOK_BLK):
        rows = slice(kk * TOK_BLK, (kk + 1) * TOK_BLK)
        blk = t * (tc // TOK_BLK) + kk
        first = meta_ref[base + blk]
        last = meta_ref[base + blk + 1]
        pcol = jnp.sum(jnp.where(mine, post_ref[0, rows, :], 0.0), axis=1, keepdims=True)
        gcol = jnp.sum(jnp.where(mine, gt_ref[0, rows, :], 0.0), axis=1, keepdims=True)

        def add_tile(j, acc, pcol=pcol):
            want = (slot + j * SCATTER_ROWS).astype(F32)
            onehot = jnp.where(pcol == want, 1.0, 0.0).astype(BF16)
            ytile = y_ref[0, 0, pl.ds(pl.multiple_of(j * SCATTER_ROWS, SCATTER_ROWS), SCATTER_ROWS), :]
            return acc + _dot(onehot, ytile)

        acc = lax.fori_loop(first // SCATTER_ROWS, (last + SCATTER_ROWS - 1) // SCATTER_ROWS, add_tile,
                            jnp.zeros((TOK_BLK, o_ref.shape[2]), F32))
        o_ref[0, rows, :] += gcol * acc

    @pl.when(e == N_EXPERTS - 1)
    def _():
        v = o_ref[0]
        ms = jnp.mean(v * v, axis=-1, keepdims=True)
        o_ref[0] = v * lax.rsqrt(ms + NORM_EPS) * gf_ref[...]


def _scatter(meta_flat, y, post, gt, x2, normf_g, tc):
    bsz, n_e, cap, d_model = y.shape
    seq = x2.shape[1]
    tc = min(tc, seq)
    gf = normf_g[None, :]
    tokw = lambda w: pl.BlockSpec((1, tc, w), lambda b, t, e, m: (b, t, 0))
    return pl.pallas_call(
        functools.partial(_scatter_body, tc=tc),
        grid_spec=pltpu.PrefetchScalarGridSpec(
            num_scalar_prefetch=1,
            grid=(bsz, seq // tc, n_e),
            in_specs=[pl.BlockSpec((1, 1, cap, d_model), lambda b, t, e, m: (b, e, 0, 0)),
                      tokw(LANES), tokw(LANES), tokw(d_model),
                      pl.BlockSpec(gf.shape, lambda b, t, e, m: (0, 0))],
            out_specs=tokw(d_model)),
        out_shape=jax.ShapeDtypeStruct((bsz, seq, d_model), F32),
        compiler_params=pltpu.CompilerParams(
            dimension_semantics=("parallel", "parallel", "arbitrary"), vmem_limit_bytes=VMEM_LIMIT),
        name="scatter",
    )(meta_flat, y, post, gt, x2, gf)


def kernel(x, norm1_g, w_in, b_gates, q_norm_g, k_norm_g, w_o_a, w_o_b, w_out, norm2_g, w_router,
           w_gate, w_up, w_down, normf_g):
    bsz, seq, d_model = x.shape
    depth = norm1_g.shape[0]
    cap = EC_CAPACITY_FACTOR * seq // N_EXPERTS
    tm = min(512, seq)
    for l in range(depth):
        qa, ka, va, qb, kb, vb, gate_a, gate_b = _in_proj(
            x, norm1_g[l], w_in[l], b_gates[l], q_norm_g[l], k_norm_g[l], tm)
        oa = _attn_a(qa, ka, va, tq=min(256, seq), tk=min(512, seq))
        obs, lses = [], []
        for gi, (_, dil) in enumerate(B_GROUPS):
            o, lse = _attn_b_group(qb, kb, vb, gi, dil, tu=512)
            obs.append(o)
            lses.append(lse)
        x2, h2, aff = _merge(x, oa, obs, lses, gate_a, gate_b, w_o_a[l], w_o_b[l], w_out[l],
                             norm2_g[l], w_router[l], tm)
        pos, post, gt, meta = _route(aff, cap)
        meta_flat = meta.reshape(-1)
        xin = _gather(meta_flat, h2, pos, cap)
        y = _ffn(xin, w_gate[l], w_up[l], w_down[l])
        assert depth == 1
        x = _scatter(meta_flat, y, post, gt, x2, normf_g, tc=2048)
    return x
```

```python
import functools

import numpy as np
import jax
import jax.numpy as jnp
from jax import lax
from jax.experimental import pallas as pl
from jax.experimental.pallas import tpu as pltpu

HEAD_DIM = 64
A_Q_HEADS = 8
A_KV_HEADS = 2
A_GROUP = A_Q_HEADS // A_KV_HEADS
B_GROUPS = ((128, 1), (512, 4), (2048, 16))
B_HEADS_PER_GROUP = 4
B_HEADS = B_HEADS_PER_GROUP * len(B_GROUPS)
GRID_W = 64
ROPE_THETA = 10000.0
N_EXPERTS = 16
EC_CAPACITY_FACTOR = 2
NORM_EPS = 1e-6
NEG_INF = -1e30

A_Q_W = A_Q_HEADS * HEAD_DIM
A_KV_W = A_KV_HEADS * HEAD_DIM
B_W = B_HEADS * HEAD_DIM
B_OUT_W = B_HEADS_PER_GROUP * HEAD_DIM

LANES = 128
MXU_DIM = 256
VMEM_LIMIT = 56 * 1024 * 1024

F32 = jnp.float32
BF16 = jnp.bfloat16
LOG2_E = 1.4426950408889634


def _nt_dot(a, b):
    return lax.dot_general(a, b, (((1,), (1,)), ((), ())), preferred_element_type=F32)


def _dot(a, b):
    return jnp.dot(a, b, preferred_element_type=F32)


def _split_bf16(v):
    hi = v.astype(BF16)
    lo = (v - hi.astype(F32)).astype(BF16)
    return hi, lo


def _in_proj_body(x_ref, g1_ref, w_ref, bg_ref, qg_ref, kg_ref, cos_ref, sin_ref, hs_ref,
                  qa_ref, ka_ref, va_ref, qb_ref, kb_ref, vb_ref, ga_ref, gb_ref, *, d_model):
    x = x_ref[0]
    ms = jnp.mean(x * x, axis=-1, keepdims=True)
    h = (x * lax.rsqrt(ms + NORM_EPS) * g1_ref[...]).astype(BF16)

    def proj(lo, width):
        return _dot(h, w_ref[:, lo:lo + width])

    cos = cos_ref[...]
    sin = sin_ref[...]
    lane = lax.broadcasted_iota(jnp.int32, (1, LANES), 1)
    low = (lane % 32) < 16

    def head_norm_rope(z, gain, nrep):
        width = LANES * nrep
        z2 = z * z
        hi, lo = _split_bf16(z2)
        hs = hs_ref[:width, :width]
        ss = _dot(hi, hs) + _dot(lo, hs)
        zn = z * lax.rsqrt(ss * (1.0 / HEAD_DIM) + NORM_EPS) * gain
        rep = lambda t: jnp.concatenate([t] * nrep, axis=1) if nrep > 1 else t
        partner = jnp.where(rep(low), pltpu.roll(zn, width - 16, 1), pltpu.roll(zn, 16, 1))
        return zn * rep(cos) + partner * rep(sin)

    scale = HEAD_DIM ** -0.5
    za = proj(0, A_Q_W + 2 * A_KV_W)
    qa = head_norm_rope(za[:, :A_Q_W], qg_ref[...], A_Q_W // LANES) * (scale * LOG2_E)
    ka = head_norm_rope(za[:, A_Q_W:A_Q_W + A_KV_W], kg_ref[...], A_KV_W // LANES)
    qa_ref[0] = qa.astype(BF16)
    ka_ref[0] = ka.astype(BF16)
    va_ref[0] = za[:, A_Q_W + A_KV_W:].astype(BF16)
    off = A_Q_W + 2 * A_KV_W
    qb_ref[0] = (proj(off, B_W) * scale).astype(BF16)
    kb_ref[0] = proj(off + B_W, B_W).astype(BF16)
    vb_ref[0] = proj(off + 2 * B_W, B_W).astype(BF16)
    off += 3 * B_W
    ga_ref[0] = jax.nn.sigmoid(proj(off, d_model) + bg_ref[:, :d_model]).astype(BF16)
    gb_ref[0] = jax.nn.sigmoid(proj(off + d_model, d_model) + bg_ref[:, d_model:]).astype(BF16)


def _rope_tables(seq):
    rows = seq // GRID_W
    row_id = jnp.repeat(jnp.arange(rows, dtype=F32), GRID_W)
    col_id = jnp.tile(jnp.arange(GRID_W, dtype=F32), rows)
    half = HEAD_DIM // 2
    inv_freq = 1.0 / (ROPE_THETA ** (jnp.arange(0, half, 2, dtype=F32) / half))
    ang_r = row_id[:, None] * inv_freq[None, :]
    ang_c = col_id[:, None] * inv_freq[None, :]
    cos = jnp.concatenate([jnp.cos(ang_r)] * 2 + [jnp.cos(ang_c)] * 2, axis=1)
    sin = jnp.concatenate([-jnp.sin(ang_r), jnp.sin(ang_r), -jnp.sin(ang_c), jnp.sin(ang_c)], axis=1)
    return jnp.tile(cos, (1, 2)), jnp.tile(sin, (1, 2))


def _in_proj(x, norm1_g, w_in, b_gates, q_norm_g, k_norm_g, tm):
    bsz, seq, d_model = x.shape
    n_in = w_in.shape[1]
    cos, sin = _rope_tables(seq)
    head_id = np.arange(A_Q_W) // HEAD_DIM
    hs = jnp.asarray(head_id[:, None] == head_id[None, :], BF16)
    qg = jnp.tile(q_norm_g, A_Q_W // HEAD_DIM)[None, :]
    kg = jnp.tile(k_norm_g, A_KV_W // HEAD_DIM)[None, :]
    row = lambda w: pl.BlockSpec((1, tm, w), lambda b, i: (b, i, 0))
    full = lambda a: pl.BlockSpec(a.shape, lambda b, i: (0,) * a.ndim)
    tab = pl.BlockSpec((tm, LANES), lambda b, i: (i, 0))
    g1 = norm1_g[None, :]
    bg = b_gates[None, :]
    w = w_in.astype(BF16)
    shp = lambda wd: jax.ShapeDtypeStruct((bsz, seq, wd), BF16)
    return pl.pallas_call(
        functools.partial(_in_proj_body, d_model=d_model),
        grid=(bsz, seq // tm),
        in_specs=[row(d_model), full(g1), full(w), full(bg), full(qg), full(kg), tab, tab, full(hs)],
        out_specs=[row(A_Q_W), row(A_KV_W), row(A_KV_W), row(B_W), row(B_W), row(B_W),
                   row(d_model), row(d_model)],
        out_shape=[shp(A_Q_W), shp(A_KV_W), shp(A_KV_W), shp(B_W), shp(B_W), shp(B_W),
                   shp(d_model), shp(d_model)],
        compiler_params=pltpu.CompilerParams(
            dimension_semantics=("parallel", "parallel"), vmem_limit_bytes=VMEM_LIMIT),
        name="in_proj",
    )(x, g1, w, bg, qg, kg, cos, sin, hs)


def _attn_a_body(q_ref, k_ref, v_ref, o_ref, qp_scr, s_scr, p_scr, m_scr, l_scr, a_scr, acc_scr, *, tk):
    tq = q_ref.shape[1]
    seq = k_ref.shape[1]
    nk = seq // tk
    rows_all = A_Q_HEADS * tq
    q = q_ref[0].astype(F32)
    zeros = jnp.zeros((tq, HEAD_DIM), F32)
    for g in range(A_Q_HEADS):
        qg = q[:, g * HEAD_DIM:(g + 1) * HEAD_DIM]
        qp = jnp.concatenate([qg, zeros] if g // A_GROUP == 0 else [zeros, qg], axis=1)
        qp_scr[g * tq:(g + 1) * tq, :] = qp.astype(BF16)
    m_scr[...] = jnp.full_like(m_scr, NEG_INF)
    l_scr[...] = jnp.zeros_like(l_scr)
    acc_scr[...] = jnp.zeros_like(acc_scr)

    def scores(kt, slot):
        start = pl.multiple_of(kt * tk, tk)
        s_scr[slot] = _nt_dot(qp_scr[...], k_ref[0, pl.ds(start, tk), :])

    def softmax_pv(kt, slot):
        for c in range(rows_all // ATTN_A_CHUNK):
            rows = slice(c * ATTN_A_CHUNK, (c + 1) * ATTN_A_CHUNK)
            s = s_scr[slot, rows, :]
            m_prev = m_scr[rows, :]
            m_new = jnp.maximum(m_prev, jnp.max(s, axis=1, keepdims=True))
            alpha = jnp.exp2(m_prev - m_new)
            p = jnp.exp2(s - jnp.concatenate([m_new] * (tk // LANES), axis=1))
            l_scr[rows, :] = alpha * l_scr[rows, :] + jnp.sum(p, axis=1, keepdims=True)
            m_scr[rows, :] = m_new
            a_scr[rows, :] = alpha
            p_scr[rows, :] = p.astype(BF16)
        start = pl.multiple_of(kt * tk, tk)
        pv = _dot(p_scr[...], v_ref[0, pl.ds(start, tk), :])
        acc_scr[...] = a_scr[...] * acc_scr[...] + pv

    scores(0, 0)

    def pair(i, carry):
        kt = 2 * i
        scores(kt + 1, 1)
        softmax_pv(kt, 0)
        scores(jnp.minimum(kt + 2, nk - 1), 0)
        softmax_pv(kt + 1, 1)
        return carry

    lax.fori_loop(0, nk // 2, pair, 0)
    outs = []
    for g in range(A_Q_HEADS):
        kv = g // A_GROUP
        rows = slice(g * tq, (g + 1) * tq)
        outs.append(acc_scr[rows, kv * HEAD_DIM:(kv + 1) * HEAD_DIM] / l_scr[rows, :HEAD_DIM])
    o_ref[0] = jnp.concatenate(outs, axis=1).astype(o_ref.dtype)


ATTN_A_CHUNK = 128


def _attn_a(qa, ka, va, tq, tk):
    bsz, seq, _ = qa.shape
    assert (seq // tk) % 2 == 0
    rows_all = A_Q_HEADS * tq
    return pl.pallas_call(
        functools.partial(_attn_a_body, tk=tk),
        grid=(bsz, seq // tq),
        in_specs=[pl.BlockSpec((1, tq, A_Q_W), lambda b, i: (b, i, 0)),
                  pl.BlockSpec((1, seq, A_KV_W), lambda b, i: (b, 0, 0)),
                  pl.BlockSpec((1, seq, A_KV_W), lambda b, i: (b, 0, 0))],
        out_specs=pl.BlockSpec((1, tq, A_Q_W), lambda b, i: (b, i, 0)),
        out_shape=jax.ShapeDtypeStruct((bsz, seq, A_Q_W), BF16),
        scratch_shapes=[pltpu.VMEM((rows_all, A_KV_W), BF16),
                        pltpu.VMEM((2, rows_all, tk), F32),
                        pltpu.VMEM((rows_all, tk), BF16),
                        pltpu.VMEM((rows_all, LANES), F32),
                        pltpu.VMEM((rows_all, LANES), F32),
                        pltpu.VMEM((rows_all, LANES), F32),
                        pltpu.VMEM((rows_all, A_KV_W), F32)],
        compiler_params=pltpu.CompilerParams(
            dimension_semantics=("parallel", "parallel"), vmem_limit_bytes=VMEM_LIMIT),
        name="attn_a",
    )(qa, ka, va)


HALO = 128
SUB = 128
N_SIDE = 64


def _attn_b_body(q_ref, kp_ref, kc_ref, kn_ref, vp_ref, vc_ref, vn_ref, o_ref, lse_ref,
                 kbuf, vbuf, *, dil, group, seq_d):
    tu = q_ref.shape[1]
    i = pl.program_id(2)
    kbuf[0:HALO] = kp_ref[0]
    kbuf[HALO:HALO + tu] = kc_ref[0]
    kbuf[HALO + tu:] = kn_ref[0]
    vbuf[0:HALO] = vp_ref[0]
    vbuf[HALO:HALO + tu] = vc_ref[0]
    vbuf[HALO + tu:] = vn_ref[0]

    win = SUB + 2 * N_SIDE
    row = lax.broadcasted_iota(jnp.int32, (SUB, win), 0)
    col = lax.broadcasted_iota(jnp.int32, (SUB, win), 1)
    rel = col - N_SIDE - row
    in_band = jnp.abs(rel) <= N_SIDE
    dist = (jnp.abs(rel) * dil).astype(F32)
    head_of_lane = lax.broadcasted_iota(jnp.int32, (1, B_OUT_W), 1) // HEAD_DIM

    for j in range(tu // SUB):
        qs = q_ref[0, j * SUB:(j + 1) * SUB, :]
        w0 = HALO - N_SIDE + j * SUB
        kw = kbuf[w0:w0 + win, :]
        vw = vbuf[w0:w0 + win, :]
        key_u = i * tu + j * SUB - N_SIDE + col
        ok = in_band & (key_u >= 0) & (key_u < seq_d)
        o_acc = jnp.zeros((SUB, B_OUT_W), F32)
        lse_acc = jnp.zeros((SUB, B_OUT_W), F32)
        for hh in range(B_HEADS_PER_GROUP):
            slope = 2.0 ** (-8.0 * (group * B_HEADS_PER_GROUP + hh + 1) / B_HEADS)
            mine = head_of_lane == hh
            qm = jnp.where(mine, qs, jnp.zeros_like(qs))
            s = _nt_dot(qm, kw) - slope * dist
            s = jnp.where(ok, s, NEG_INF)
            m = jnp.max(s, axis=1, keepdims=True)
            p = jnp.exp(s - m)
            l = jnp.sum(p, axis=1, keepdims=True)
            pv = _dot(p.astype(BF16), vw)
            o_acc = jnp.where(mine, pv / l, o_acc)
            lse_acc = jnp.where(mine, m + jnp.log(l), lse_acc)
        o_ref[0, j * SUB:(j + 1) * SUB, :] = o_acc
        lse_ref[0, j * SUB:(j + 1) * SUB, :] = lse_acc


def _attn_b_group(qb, kb, vb, group, dil, tu):
    bsz, seq, _ = qb.shape
    seq_d = seq // dil
    tu = min(tu, seq_d)
    per = tu // HALO
    last = seq_d // HALO - 1
    view = lambda a: a.reshape(bsz, seq_d, dil * B_W)
    ncol = B_W // B_OUT_W
    colb = lambda r: r * ncol + group
    cur = pl.BlockSpec((1, tu, B_OUT_W), lambda b, r, i: (b, i, colb(r)))
    prev = pl.BlockSpec((1, HALO, B_OUT_W), lambda b, r, i: (b, jnp.maximum(i * per - 1, 0), colb(r)))
    nxt = pl.BlockSpec((1, HALO, B_OUT_W), lambda b, r, i: (b, jnp.minimum((i + 1) * per, last), colb(r)))
    out = pl.BlockSpec((1, tu, B_OUT_W), lambda b, r, i: (b, i, r))
    o, lse = pl.pallas_call(
        functools.partial(_attn_b_body, dil=dil, group=group, seq_d=seq_d),
        grid=(bsz, dil, seq_d // tu),
        in_specs=[cur, prev, cur, nxt, prev, cur, nxt],
        out_specs=[out, out],
        out_shape=[jax.ShapeDtypeStruct((bsz, seq_d, dil * B_OUT_W), F32)] * 2,
        scratch_shapes=[pltpu.VMEM((tu + 2 * HALO, B_OUT_W), BF16)] * 2,
        compiler_params=pltpu.CompilerParams(
            dimension_semantics=("parallel", "parallel", "parallel"), vmem_limit_bytes=VMEM_LIMIT),
        name=f"attn_b{group}",
    )(view(qb), view(kb), view(kb), view(kb), view(vb), view(vb), view(vb))
    return o.reshape(bsz, seq, B_OUT_W), lse.reshape(bsz, seq, B_OUT_W)


def _merge_body(x_ref, oa_ref, o0_ref, o1_ref, o2_ref, l0_ref, l1_ref, l2_ref, ga_ref, gb_ref,
                woa_ref, wob_ref, wout_ref, g2_ref, wrh_ref, wrl_ref,
                x2_ref, h2_ref, aff_ref):
    l0, l1, l2 = l0_ref[0], l1_ref[0], l2_ref[0]
    mx = jnp.maximum(jnp.maximum(l0, l1), l2)
    e0, e1, e2 = jnp.exp(l0 - mx), jnp.exp(l1 - mx), jnp.exp(l2 - mx)
    ob = (e0 * o0_ref[0] + e1 * o1_ref[0] + e2 * o2_ref[0]) / (e0 + e1 + e2)
    ya = _dot(oa_ref[0], woa_ref[...])
    yb = _dot(ob.astype(BF16), wob_ref[...])
    mrg = ga_ref[0].astype(F32) * ya + gb_ref[0].astype(F32) * yb
    x2 = x_ref[0] + _dot(mrg.astype(BF16), wout_ref[...])
    x2_ref[0] = x2
    ms = jnp.mean(x2 * x2, axis=-1, keepdims=True)
    h2 = x2 * lax.rsqrt(ms + NORM_EPS) * g2_ref[...]
    h2_ref[0] = h2.astype(BF16)
    hh, hl = _split_bf16(h2)
    wh, wl = wrh_ref[...], wrl_ref[...]
    logits = _nt_dot(wh, hh) + _nt_dot(wh, hl) + _nt_dot(wl, hh)
    mxl = jnp.max(logits, axis=0, keepdims=True)
    ex = jnp.exp(logits - mxl)
    aff_ref[0] = ex / jnp.sum(ex, axis=0, keepdims=True)


def _merge(x, oa, obs, lses, gate_a, gate_b, w_o_a, w_o_b, w_out, norm2_g, w_router, tm):
    bsz, seq, d_model = x.shape
    row = lambda w: pl.BlockSpec((1, tm, w), lambda b, i: (b, i, 0))
    full = lambda a: pl.BlockSpec(a.shape, lambda b, i: (0,) * a.ndim)
    woa, wob, wout = w_o_a.astype(BF16), w_o_b.astype(BF16), w_out.astype(BF16)
    g2 = norm2_g[None, :]
    wrh, wrl = _split_bf16(w_router.T)
    return pl.pallas_call(
        _merge_body,
        grid=(bsz, seq // tm),
        in_specs=[row(d_model), row(A_Q_W)] + [row(B_OUT_W)] * 6 + [row(d_model)] * 2
                 + [full(woa), full(wob), full(wout), full(g2), full(wrh), full(wrl)],
        out_specs=[row(d_model), row(d_model),
                   pl.BlockSpec((1, N_EXPERTS, tm), lambda b, i: (b, 0, i))],
        out_shape=[jax.ShapeDtypeStruct((bsz, seq, d_model), F32),
                   jax.ShapeDtypeStruct((bsz, seq, d_model), BF16),
                   jax.ShapeDtypeStruct((bsz, N_EXPERTS, seq), F32)],
        compiler_params=pltpu.CompilerParams(
            dimension_semantics=("parallel", "parallel"), vmem_limit_bytes=VMEM_LIMIT),
        name="merge",
    )(x, oa, *obs, *lses, gate_a, gate_b, woa, wob, wout, g2, wrh, wrl)


TOK_BLK = 256
GATHER_ROWS = 128
META_LO = 64
META_HI = 96


def _route_body(aff_ref, pos_ref, post_ref, gt_ref, meta_ref, *, cap):
    aff = aff_ref[0]
    n_e, seq = aff.shape
    bits = pltpu.bitcast(aff, jnp.int32)

    def refine(it, thr):
        cand = thr | jnp.left_shift(jnp.int32(1), 30 - it)
        cnt = jnp.sum(jnp.where(bits >= cand, 1.0, 0.0), axis=1, keepdims=True)
        return jnp.where(cnt >= cap, cand, thr)

    thr = lax.fori_loop(0, 31, refine, jnp.zeros((n_e, 1), jnp.int32))
    above = bits > thr
    tied = bits == thr
    need = cap - jnp.sum(jnp.where(above, 1.0, 0.0), axis=1, keepdims=True)

    ri = lax.broadcasted_iota(jnp.int32, (TOK_BLK, TOK_BLK), 0)
    ci = lax.broadcasted_iota(jnp.int32, (TOK_BLK, TOK_BLK), 1)
    before = jnp.where(ri < ci, 1.0, 0.0).astype(BF16)
    lane = lax.broadcasted_iota(jnp.int32, (1, LANES), 1)
    pad_rows = LANES - n_e

    run_tied = jnp.zeros((n_e, 1), F32)
    run_sel = jnp.zeros((n_e, 1), F32)
    starts = jnp.zeros((n_e, LANES), F32)
    ends = jnp.zeros((n_e, LANES), F32)
    n_blk = seq // TOK_BLK
    for c in range(n_blk):
        sl = slice(c * TOK_BLK, (c + 1) * TOK_BLK)
        tied_c = jnp.where(tied[:, sl], 1.0, 0.0)
        rank = run_tied + _dot(tied_c.astype(BF16), before)
        sel = above[:, sl] | (tied[:, sl] & (rank < need))
        sel_c = jnp.where(sel, 1.0, 0.0)
        pos = run_sel + _dot(sel_c.astype(BF16), before)
        pos_m = jnp.where(sel, pos, -1.0)
        g_m = jnp.where(sel, aff[:, sl], 0.0)
        pos_ref[0, :, sl] = pos_m
        fill = jnp.full((pad_rows, TOK_BLK), -1.0, F32)
        post_ref[0, sl, :] = jnp.concatenate([pos_m, fill], axis=0).T
        gt_ref[0, sl, :] = jnp.concatenate([g_m, jnp.zeros((pad_rows, TOK_BLK), F32)], axis=0).T
        starts = jnp.where(lane == c, run_sel, starts)
        run_tied = run_tied + jnp.sum(tied_c, axis=1, keepdims=True)
        run_sel = run_sel + jnp.sum(sel_c, axis=1, keepdims=True)
        ends = jnp.where(lane == c, run_sel, ends)
    meta = jnp.where(lane == n_blk, run_sel, starts)
    real = lane < n_blk
    for j in range(cap // GATHER_ROWS):
        c0 = float(j * GATHER_ROWS)
        first = jnp.sum(jnp.where(real & (ends <= c0), 1.0, 0.0), axis=1, keepdims=True)
        stop = n_blk - jnp.sum(jnp.where(real & (starts >= c0 + GATHER_ROWS), 1.0, 0.0), axis=1, keepdims=True)
        meta = jnp.where(lane == META_LO + j, first, meta)
        meta = jnp.where(lane == META_HI + j, stop, meta)
    meta_ref[0] = meta.astype(jnp.int32)


def _route(aff, cap):
    bsz, n_e, seq = aff.shape
    return pl.pallas_call(
        functools.partial(_route_body, cap=cap),
        grid=(bsz,),
        in_specs=[pl.BlockSpec((1, n_e, seq), lambda b: (b, 0, 0))],
        out_specs=[pl.BlockSpec((1, n_e, seq), lambda b: (b, 0, 0)),
                   pl.BlockSpec((1, seq, LANES), lambda b: (b, 0, 0)),
                   pl.BlockSpec((1, seq, LANES), lambda b: (b, 0, 0)),
                   pl.BlockSpec((1, n_e, LANES), lambda b: (b, 0, 0))],
        out_shape=[jax.ShapeDtypeStruct((bsz, n_e, seq), F32),
                   jax.ShapeDtypeStruct((bsz, seq, LANES), F32),
                   jax.ShapeDtypeStruct((bsz, seq, LANES), F32),
                   jax.ShapeDtypeStruct((bsz, n_e, LANES), jnp.int32)],
        compiler_params=pltpu.CompilerParams(
            dimension_semantics=("parallel",), vmem_limit_bytes=VMEM_LIMIT),
        name="route",
    )(aff)


def _gather_body(meta_ref, h_ref, pos_ref, xin_ref, acc_ref, *, cap):
    b, e = pl.program_id(0), pl.program_id(1)
    base = (b * N_EXPERTS + e) * LANES
    slot = lax.broadcasted_iota(jnp.int32, (GATHER_ROWS, 1), 0)
    for j in range(cap // GATHER_ROWS):
        want = (slot + j * GATHER_ROWS).astype(F32)
        acc_ref[...] = jnp.zeros_like(acc_ref)

        def add_block(k, carry, want=want):
            pos = pos_ref[0, 0, pl.ds(k, 1), :]
            onehot = jnp.where(pos == want, 1.0, 0.0).astype(BF16)
            hblk = h_ref[0, pl.ds(pl.multiple_of(k * TOK_BLK, TOK_BLK), TOK_BLK), :]
            acc_ref[...] += _dot(onehot, hblk)
            return carry

        lax.fori_loop(meta_ref[base + META_LO + j], meta_ref[base + META_HI + j], add_block, 0)
        xin_ref[0, 0, j * GATHER_ROWS:(j + 1) * GATHER_ROWS, :] = acc_ref[...].astype(xin_ref.dtype)


def _gather(meta_flat, h2, pos, cap):
    bsz, seq, d_model = h2.shape
    n_e = pos.shape[1]
    pos4 = pos.reshape(bsz, n_e, seq // TOK_BLK, TOK_BLK)
    return pl.pallas_call(
        functools.partial(_gather_body, cap=cap),
        grid_spec=pltpu.PrefetchScalarGridSpec(
            num_scalar_prefetch=1,
            grid=(bsz, n_e),
            in_specs=[pl.BlockSpec((1, seq, d_model), lambda b, e, m: (b, 0, 0)),
                      pl.BlockSpec((1, 1, seq // TOK_BLK, TOK_BLK), lambda b, e, m: (b, e, 0, 0))],
            out_specs=pl.BlockSpec((1, 1, cap, d_model), lambda b, e, m: (b, e, 0, 0)),
            scratch_shapes=[pltpu.VMEM((GATHER_ROWS, d_model), F32)]),
        out_shape=jax.ShapeDtypeStruct((bsz, n_e, cap, d_model), BF16),
        compiler_params=pltpu.CompilerParams(
            dimension_semantics=("parallel", "parallel"), vmem_limit_bytes=VMEM_LIMIT),
        name="gather",
    )(meta_flat, h2, pos4)


FFN_ROWS = 256


def _ffn_body(x_ref, wg_ref, wu_ref, wd_ref, y_ref):
    cap = x_ref.shape[2]
    for r in range(cap // FFN_ROWS):
        rows = slice(r * FFN_ROWS, (r + 1) * FFN_ROWS)
        xin = x_ref[0, 0, rows, :]
        a = _dot(xin, wg_ref[0])
        u = _dot(xin, wu_ref[0])
        mid = (a * jax.nn.sigmoid(a) * u).astype(BF16)
        y_ref[0, 0, rows, :] = _dot(mid, wd_ref[0]).astype(y_ref.dtype)


def _ffn(xin, w_gate, w_up, w_down):
    bsz, n_e, cap, d_model = xin.shape
    ff = w_gate.shape[2]
    tok = pl.BlockSpec((1, 1, cap, d_model), lambda e, b: (b, e, 0, 0))
    return pl.pallas_call(
        _ffn_body,
        grid=(n_e, bsz),
        in_specs=[tok,
                  pl.BlockSpec((1, d_model, ff), lambda e, b: (e, 0, 0)),
                  pl.BlockSpec((1, d_model, ff), lambda e, b: (e, 0, 0)),
                  pl.BlockSpec((1, ff, d_model), lambda e, b: (e, 0, 0))],
        out_specs=tok,
        out_shape=jax.ShapeDtypeStruct((bsz, n_e, cap, d_model), BF16),
        compiler_params=pltpu.CompilerParams(
            dimension_semantics=("parallel", "parallel"), vmem_limit_bytes=VMEM_LIMIT),
        name="ffn",
    )(xin, w_gate.astype(BF16), w_up.astype(BF16), w_down.astype(BF16))


SCATTER_ROWS = 256


def _scatter_body(meta_ref, y_ref, post_ref, gt_ref, x2_ref, gf_ref, o_ref, *, tc):
    b, t, e = pl.program_id(0), pl.program_id(1), pl.program_id(2)
    base = (b * N_EXPERTS + e) * LANES

    @pl.when(e == 0)
    def _():
        o_ref[0] = x2_ref[0]

    lane = lax.broadcasted_iota(jnp.int32, (1, LANES), 1)
    mine = lane == e
    slot = lax.broadcasted_iota(jnp.int32, (1, SCATTER_ROWS), 1)
    for kk in range(tc // TOK_BLK):
        rows = slice(kk * TOK_BLK, (kk + 1) * TOK_BLK)
        blk = t * (tc // TOK_BLK) + kk
        first = meta_ref[base + blk]
        last = meta_ref[base + blk + 1]
        pcol = jnp.sum(jnp.where(mine, post_ref[0, rows, :], 0.0), axis=1, keepdims=True)
        gcol = jnp.sum(jnp.where(mine, gt_ref[0, rows, :], 0.0), axis=1, keepdims=True)

        def add_tile(j, acc, pcol=pcol):
            want = (slot + j * SCATTER_ROWS).astype(F32)
            onehot = jnp.where(pcol == want, 1.0, 0.0).astype(BF16)
            ytile = y_ref[0, 0, pl.ds(pl.multiple_of(j * SCATTER_ROWS, SCATTER_ROWS), SCATTER_ROWS), :]
            return acc + _dot(onehot, ytile)

        acc = lax.fori_loop(first // SCATTER_ROWS, (last + SCATTER_ROWS - 1) // SCATTER_ROWS, add_tile,
                            jnp.zeros((TOK_BLK, o_ref.shape[2]), F32))
        o_ref[0, rows, :] += gcol * acc

    @pl.when(e == N_EXPERTS - 1)
    def _():
        v = o_ref[0]
        ms = jnp.mean(v * v, axis=-1, keepdims=True)
        o_ref[0] = v * lax.rsqrt(ms + NORM_EPS) * gf_ref[...]


def _scatter(meta_flat, y, post, gt, x2, normf_g, tc):
    bsz, n_e, cap, d_model = y.shape
    seq = x2.shape[1]
    tc = min(tc, seq)
    gf = normf_g[None, :]
    tokw = lambda w: pl.BlockSpec((1, tc, w), lambda b, t, e, m: (b, t, 0))
    return pl.pallas_call(
        functools.partial(_scatter_body, tc=tc),
        grid_spec=pltpu.PrefetchScalarGridSpec(
            num_scalar_prefetch=1,
            grid=(bsz, seq // tc, n_e),
            in_specs=[pl.BlockSpec((1, 1, cap, d_model), lambda b, t, e, m: (b, e, 0, 0)),
                      tokw(LANES), tokw(LANES), tokw(d_model),
                      pl.BlockSpec(gf.shape, lambda b, t, e, m: (0, 0))],
            out_specs=tokw(d_model)),
        out_shape=jax.ShapeDtypeStruct((bsz, seq, d_model), F32),
        compiler_params=pltpu.CompilerParams(
            dimension_semantics=("parallel", "parallel", "arbitrary"), vmem_limit_bytes=VMEM_LIMIT),
        name="scatter",
    )(meta_flat, y, post, gt, x2, gf)


def kernel(x, norm1_g, w_in, b_gates, q_norm_g, k_norm_g, w_o_a, w_o_b, w_out, norm2_g, w_router,
           w_gate, w_up, w_down, normf_g):
    bsz, seq, d_model = x.shape
    depth = norm1_g.shape[0]
    cap = EC_CAPACITY_FACTOR * seq // N_EXPERTS
    tm = min(512, seq)
    for l in range(depth):
        qa, ka, va, qb, kb, vb, gate_a, gate_b = _in_proj(
            x, norm1_g[l], w_in[l], b_gates[l], q_norm_g[l], k_norm_g[l], tm)
        oa = _attn_a(qa, ka, va, tq=min(128, seq), tk=min(512, seq))
        obs, lses = [], []
        for gi, (_, dil) in enumerate(B_GROUPS):
            o, lse = _attn_b_group(qb, kb, vb, gi, dil, tu=512)
            obs.append(o)
            lses.append(lse)
        x2, h2, aff = _merge(x, oa, obs, lses, gate_a, gate_b, w_o_a[l], w_o_b[l], w_out[l],
                             norm2_g[l], w_router[l], tm)
        pos, post, gt, meta = _route(aff, cap)
        meta_flat = meta.reshape(-1)
        xin = _gather(meta_flat, h2, pos, cap)
        y = _ffn(xin, w_gate[l], w_up[l], w_down[l])
        assert depth == 1
        x = _scatter(meta_flat, y, post, gt, x2, normf_g, tc=2048)
    return x
```

```python
import functools

import numpy as np
import jax
import jax.numpy as jnp
from jax import lax
from jax.experimental import pallas as pl
from jax.experimental.pallas import tpu as pltpu

HEAD_DIM = 64
A_Q_HEADS = 8
A_KV_HEADS = 2
A_GROUP = A_Q_HEADS // A_KV_HEADS
B_GROUPS = ((128, 1), (512, 4), (2048, 16))
B_HEADS_PER_GROUP = 4
B_HEADS = B_HEADS_PER_GROUP * len(B_GROUPS)
GRID_W = 64
ROPE_THETA = 10000.0
N_EXPERTS = 16
EC_CAPACITY_FACTOR = 2
NORM_EPS = 1e-6
NEG_INF = -1e30

A_Q_W = A_Q_HEADS * HEAD_DIM
A_KV_W = A_KV_HEADS * HEAD_DIM
B_W = B_HEADS * HEAD_DIM
B_OUT_W = B_HEADS_PER_GROUP * HEAD_DIM

LANES = 128
MXU_DIM = 256
VMEM_LIMIT = 56 * 1024 * 1024

F32 = jnp.float32
BF16 = jnp.bfloat16
LOG2_E = 1.4426950408889634


def _nt_dot(a, b):
    return lax.dot_general(a, b, (((1,), (1,)), ((), ())), preferred_element_type=F32)


def _dot(a, b):
    return jnp.dot(a, b, preferred_element_type=F32)


def _split_bf16(v):
    hi = v.astype(BF16)
    lo = (v - hi.astype(F32)).astype(BF16)
    return hi, lo


def _in_proj_body(x_ref, g1_ref, w_ref, bg_ref, qg_ref, kg_ref, cos_ref, sin_ref, hs_ref,
                  qa_ref, ka_ref, va_ref, ga_ref, gb_ref, *rest, d_model):
    b_refs, zb_scr = rest[:-1], rest[-1]
    tm = x_ref.shape[1]
    x = x_ref[0]
    ms = jnp.mean(x * x, axis=-1, keepdims=True)
    h = (x * lax.rsqrt(ms + NORM_EPS) * g1_ref[...]).astype(BF16)

    def proj(lo, width):
        return _dot(h, w_ref[:, lo:lo + width])

    cos = cos_ref[...]
    sin = sin_ref[...]
    lane = lax.broadcasted_iota(jnp.int32, (1, LANES), 1)
    low = (lane % 32) < 16

    def head_norm_rope(z, gain, nrep):
        width = LANES * nrep
        z2 = z * z
        hi, lo = _split_bf16(z2)
        hs = hs_ref[:width, :width]
        ss = _dot(hi, hs) + _dot(lo, hs)
        zn = z * lax.rsqrt(ss * (1.0 / HEAD_DIM) + NORM_EPS) * gain
        rep = lambda t: jnp.concatenate([t] * nrep, axis=1) if nrep > 1 else t
        partner = jnp.where(rep(low), pltpu.roll(zn, width - 16, 1), pltpu.roll(zn, 16, 1))
        return zn * rep(cos) + partner * rep(sin)

    scale = HEAD_DIM ** -0.5
    za = proj(0, A_Q_W + 2 * A_KV_W)
    qa = head_norm_rope(za[:, :A_Q_W], qg_ref[...], A_Q_W // LANES) * (scale * LOG2_E)
    ka = head_norm_rope(za[:, A_Q_W:A_Q_W + A_KV_W], kg_ref[...], A_KV_W // LANES)
    qa_ref[0] = qa.astype(BF16)
    ka_ref[0] = ka.astype(BF16)
    va = za[:, A_Q_W + A_KV_W:]
    va_ref[0] = jnp.concatenate([va, jnp.ones_like(va)], axis=1).astype(BF16)
    off = A_Q_W + 2 * A_KV_W
    slabs = B_OUT_W // LANES
    for t in range(3):
        z = proj(off + t * B_W, B_W) * (scale if t == 0 else 1.0)
        for c in range(B_W // LANES):
            zb_scr[c] = z[:, c * LANES:(c + 1) * LANES]
        for gi, (_, dil) in enumerate(B_GROUPS):
            out = b_refs[3 * gi + t]
            for r in range(dil):
                for c in range(slabs):
                    out[0, r, :, c * LANES:(c + 1) * LANES] = (
                        zb_scr[gi * slabs + c, pl.ds(r, tm // dil, stride=dil), :].astype(BF16))
    off += 3 * B_W
    ga_ref[0] = jax.nn.sigmoid(proj(off, d_model) + bg_ref[:, :d_model]).astype(BF16)
    gb_ref[0] = jax.nn.sigmoid(proj(off + d_model, d_model) + bg_ref[:, d_model:]).astype(BF16)


def _rope_tables(seq):
    rows = seq // GRID_W
    row_id = jnp.repeat(jnp.arange(rows, dtype=F32), GRID_W)
    col_id = jnp.tile(jnp.arange(GRID_W, dtype=F32), rows)
    half = HEAD_DIM // 2
    inv_freq = 1.0 / (ROPE_THETA ** (jnp.arange(0, half, 2, dtype=F32) / half))
    ang_r = row_id[:, None] * inv_freq[None, :]
    ang_c = col_id[:, None] * inv_freq[None, :]
    cos = jnp.concatenate([jnp.cos(ang_r)] * 2 + [jnp.cos(ang_c)] * 2, axis=1)
    sin = jnp.concatenate([-jnp.sin(ang_r), jnp.sin(ang_r), -jnp.sin(ang_c), jnp.sin(ang_c)], axis=1)
    return jnp.tile(cos, (1, 2)), jnp.tile(sin, (1, 2))


def _in_proj(x, norm1_g, w_in, b_gates, q_norm_g, k_norm_g, tm):
    bsz, seq, d_model = x.shape
    cos, sin = _rope_tables(seq)
    head_id = np.arange(A_Q_W) // HEAD_DIM
    hs = jnp.asarray(head_id[:, None] == head_id[None, :], BF16)
    qg = jnp.tile(q_norm_g, A_Q_W // HEAD_DIM)[None, :]
    kg = jnp.tile(k_norm_g, A_KV_W // HEAD_DIM)[None, :]
    row = lambda w: pl.BlockSpec((1, tm, w), lambda b, i: (b, i, 0))
    full = lambda a: pl.BlockSpec(a.shape, lambda b, i: (0,) * a.ndim)
    tab = pl.BlockSpec((tm, LANES), lambda b, i: (i, 0))
    g1 = norm1_g[None, :]
    bg = b_gates[None, :]
    w = w_in.astype(BF16)
    shp = lambda wd: jax.ShapeDtypeStruct((bsz, seq, wd), BF16)
    b_specs, b_shapes = [], []
    for _, dil in B_GROUPS:
        assert tm % (16 * dil) == 0
        b_specs += [pl.BlockSpec((1, dil, tm // dil, B_OUT_W), lambda b, i: (b, 0, i, 0))] * 3
        b_shapes += [jax.ShapeDtypeStruct((bsz, dil, seq // dil, B_OUT_W), BF16)] * 3
    outs = pl.pallas_call(
        functools.partial(_in_proj_body, d_model=d_model),
        grid=(bsz, seq // tm),
        in_specs=[row(d_model), full(g1), full(w), full(bg), full(qg), full(kg), tab, tab, full(hs)],
        out_specs=[row(A_Q_W), row(A_KV_W), row(2 * A_KV_W), row(d_model), row(d_model)] + b_specs,
        out_shape=[shp(A_Q_W), shp(A_KV_W), shp(2 * A_KV_W), shp(d_model), shp(d_model)] + b_shapes,
        scratch_shapes=[pltpu.VMEM((B_W // LANES, tm, LANES), F32)],
        compiler_params=pltpu.CompilerParams(
            dimension_semantics=("parallel", "parallel"), vmem_limit_bytes=VMEM_LIMIT),
        name="in_proj",
    )(x, g1, w, bg, qg, kg, cos, sin, hs)
    qa, ka, va, gate_a, gate_b = outs[:5]
    qkv_b = [outs[5 + 3 * gi:8 + 3 * gi] for gi in range(len(B_GROUPS))]
    return qa, ka, va, gate_a, gate_b, qkv_b


def _attn_a_body(q_ref, k_ref, v_ref, o_ref, qp_scr, s_scr, p_scr, m_scr, a_scr, acc_scr, *, tk):
    tq = q_ref.shape[1]
    seq = k_ref.shape[1]
    nk = seq // tk
    rows_all = A_Q_HEADS * tq
    q = q_ref[0].astype(F32)
    zeros = jnp.zeros((tq, HEAD_DIM), F32)
    for g in range(A_Q_HEADS):
        qg = q[:, g * HEAD_DIM:(g + 1) * HEAD_DIM]
        qp = jnp.concatenate([qg, zeros] if g // A_GROUP == 0 else [zeros, qg], axis=1)
        qp_scr[g * tq:(g + 1) * tq, :] = qp.astype(BF16)
    m_scr[...] = jnp.full_like(m_scr, NEG_INF)
    acc_scr[...] = jnp.zeros_like(acc_scr)

    def scores(kt, slot):
        start = pl.multiple_of(kt * tk, tk)
        s_scr[slot] = _nt_dot(qp_scr[...], k_ref[0, pl.ds(start, tk), :])

    def softmax_pv(kt, slot):
        for c in range(rows_all // ATTN_A_CHUNK):
            rows = slice(c * ATTN_A_CHUNK, (c + 1) * ATTN_A_CHUNK)
            s = s_scr[slot, rows, :]
            m_prev = m_scr[rows, :]
            m_new = jnp.maximum(m_prev, jnp.max(s, axis=1, keepdims=True))
            m_scr[rows, :] = m_new
            a_scr[rows, :] = jnp.exp2(m_prev - m_new)
            d = s - jnp.concatenate([m_new] * (tk // LANES), axis=1)
            p_scr[rows, :] = jnp.exp2(d.astype(BF16))
        start = pl.multiple_of(kt * tk, tk)
        pv = _dot(p_scr[...], v_ref[0, pl.ds(start, tk), :])
        alpha = a_scr[...]
        acc_scr[...] = jnp.concatenate([alpha, alpha], axis=1) * acc_scr[...] + pv

    scores(0, 0)

    def pair(i, carry):
        kt = 2 * i
        scores(kt + 1, 1)
        softmax_pv(kt, 0)
        scores(jnp.minimum(kt + 2, nk - 1), 0)
        softmax_pv(kt + 1, 1)
        return carry

    lax.fori_loop(0, nk // 2, pair, 0)
    outs = []
    for g in range(A_Q_HEADS):
        kv = g // A_GROUP
        rows = slice(g * tq, (g + 1) * tq)
        outs.append(acc_scr[rows, kv * HEAD_DIM:(kv + 1) * HEAD_DIM]
                    / acc_scr[rows, A_KV_W:A_KV_W + HEAD_DIM])
    o_ref[0] = jnp.concatenate(outs, axis=1).astype(o_ref.dtype)


ATTN_A_CHUNK = 128


def _attn_a(qa, ka, va, tq, tk):
    bsz, seq, _ = qa.shape
    assert (seq // tk) % 2 == 0
    rows_all = A_Q_HEADS * tq
    return pl.pallas_call(
        functools.partial(_attn_a_body, tk=tk),
        grid=(bsz, seq // tq),
        in_specs=[pl.BlockSpec((1, tq, A_Q_W), lambda b, i: (b, i, 0)),
                  pl.BlockSpec((1, seq, A_KV_W), lambda b, i: (b, 0, 0)),
                  pl.BlockSpec((1, seq, 2 * A_KV_W), lambda b, i: (b, 0, 0))],
        out_specs=pl.BlockSpec((1, tq, A_Q_W), lambda b, i: (b, i, 0)),
        out_shape=jax.ShapeDtypeStruct((bsz, seq, A_Q_W), BF16),
        scratch_shapes=[pltpu.VMEM((rows_all, A_KV_W), BF16),
                        pltpu.VMEM((2, rows_all, tk), F32),
                        pltpu.VMEM((rows_all, tk), BF16),
                        pltpu.VMEM((rows_all, LANES), F32),
                        pltpu.VMEM((rows_all, LANES), F32),
                        pltpu.VMEM((rows_all, 2 * A_KV_W), F32)],
        compiler_params=pltpu.CompilerParams(
            dimension_semantics=("parallel", "parallel"), vmem_limit_bytes=VMEM_LIMIT),
        name="attn_a",
    )(qa, ka, va)


HALO = 128
SUB = 128
N_SIDE = 64
WIN = SUB + 2 * N_SIDE


def _attn_b_body(q_ref, kp_ref, kc_ref, kn_ref, vp_ref, vc_ref, vn_ref, bias_ref, o_ref, lse_ref,
                 kbuf, vbuf, *, seq_d):
    tu = q_ref.shape[2]
    i = pl.program_id(2)
    kbuf[0:HALO] = kp_ref[0, 0]
    kbuf[HALO:HALO + tu] = kc_ref[0, 0]
    kbuf[HALO + tu:] = kn_ref[0, 0]
    vbuf[0:HALO] = vp_ref[0, 0]
    vbuf[HALO:HALO + tu] = vc_ref[0, 0]
    vbuf[HALO + tu:] = vn_ref[0, 0]

    nh = B_HEADS_PER_GROUP
    nsub = tu // SUB
    head_of_lane = lax.broadcasted_iota(jnp.int32, (1, B_OUT_W), 1) // HEAD_DIM
    col = lax.broadcasted_iota(jnp.int32, (1, WIN), 1)

    def scores(j):
        qs = q_ref[0, 0, j * SUB:(j + 1) * SUB, :]
        qst = jnp.concatenate([jnp.where(head_of_lane == hh, qs, jnp.zeros_like(qs)) for hh in range(nh)],
                              axis=0)
        w0 = HALO - N_SIDE + j * SUB
        return _nt_dot(qst, kbuf[w0:w0 + WIN, :])

    s_next = scores(0)
    for j in range(nsub):
        s = s_next + bias_ref[...]
        if j + 1 < nsub:
            s_next = scores(j + 1)
        if j == 0 or j == nsub - 1:
            key_u = i * tu + j * SUB - N_SIDE + col
            s = jnp.where((key_u >= 0) & (key_u < seq_d), s, NEG_INF)
        m = jnp.max(s, axis=1, keepdims=True)
        p = jnp.exp(s - m)
        l = jnp.sum(p, axis=1, keepdims=True)
        w0 = HALO - N_SIDE + j * SUB
        pv = _dot(p.astype(BF16), vbuf[w0:w0 + WIN, :])
        on = pv / l
        lse = m + jnp.log(l)
        o_acc = jnp.zeros((SUB, B_OUT_W), F32)
        lse_acc = jnp.zeros((SUB, B_OUT_W), F32)
        for hh in range(nh):
            rows = slice(hh * SUB, (hh + 1) * SUB)
            mine = head_of_lane == hh
            o_acc = jnp.where(mine, on[rows], o_acc)
            lse_acc = jnp.where(mine, lse[rows], lse_acc)
        o_ref[0, 0, j * SUB:(j + 1) * SUB, :] = o_acc
        lse_ref[0, 0, j * SUB:(j + 1) * SUB, :] = lse_acc


def _band_bias(group, dil):
    row = np.arange(SUB)[:, None]
    col = np.arange(WIN)[None, :]
    rel = col - N_SIDE - row
    tiles = []
    for hh in range(B_HEADS_PER_GROUP):
        slope = 2.0 ** (-8.0 * (group * B_HEADS_PER_GROUP + hh + 1) / B_HEADS)
        tiles.append(np.where(np.abs(rel) <= N_SIDE, -slope * np.abs(rel) * dil, NEG_INF))
    return jnp.asarray(np.concatenate(tiles, axis=0), F32)


def _attn_b_group(q, k, v, group, dil, tu):
    bsz, _, seq_d, _ = q.shape
    tu = min(tu, seq_d)
    per = tu // HALO
    last = seq_d // HALO - 1
    bias = _band_bias(group, dil)
    cur = pl.BlockSpec((1, 1, tu, B_OUT_W), lambda b, r, i: (b, r, i, 0))
    prev = pl.BlockSpec((1, 1, HALO, B_OUT_W), lambda b, r, i: (b, r, jnp.maximum(i * per - 1, 0), 0))
    nxt = pl.BlockSpec((1, 1, HALO, B_OUT_W), lambda b, r, i: (b, r, jnp.minimum((i + 1) * per, last), 0))
    return pl.pallas_call(
        functools.partial(_attn_b_body, seq_d=seq_d),
        grid=(bsz, dil, seq_d // tu),
        in_specs=[cur, prev, cur, nxt, prev, cur, nxt,
                  pl.BlockSpec(bias.shape, lambda b, r, i: (0, 0))],
        out_specs=[cur, cur],
        out_shape=[jax.ShapeDtypeStruct((bsz, dil, seq_d, B_OUT_W), F32)] * 2,
        scratch_shapes=[pltpu.VMEM((tu + 2 * HALO, B_OUT_W), BF16)] * 2,
        compiler_params=pltpu.CompilerParams(
            dimension_semantics=("parallel", "parallel", "parallel"), vmem_limit_bytes=VMEM_LIMIT),
        name=f"attn_b{group}",
    )(q, k, k, k, v, v, v, bias)


def _merge_body(x_ref, oa_ref, o0_ref, o1_ref, o2_ref, l0_ref, l1_ref, l2_ref, ga_ref, gb_ref,
                woa_ref, wob_ref, wout_ref, g2_ref, wrh_ref, wrl_ref,
                x2_ref, h2_ref, aff_ref, o_scr, l_scr):
    tm = x_ref.shape[1]
    slabs = B_OUT_W // LANES
    for gi, (o_ref, lse_ref) in enumerate(((o0_ref, l0_ref), (o1_ref, l1_ref), (o2_ref, l2_ref))):
        dil = B_GROUPS[gi][1]
        for r in range(dil):
            for c in range(slabs):
                lanes = slice(c * LANES, (c + 1) * LANES)
                o_scr[gi * slabs + c, pl.ds(r, tm // dil, stride=dil), :] = o_ref[0, r, :, lanes]
                l_scr[gi * slabs + c, pl.ds(r, tm // dil, stride=dil), :] = lse_ref[0, r, :, lanes]
    whole = lambda scr, gi: jnp.concatenate([scr[gi * slabs + c] for c in range(slabs)], axis=1)
    l0, l1, l2 = whole(l_scr, 0), whole(l_scr, 1), whole(l_scr, 2)
    mx = jnp.maximum(jnp.maximum(l0, l1), l2)
    e0, e1, e2 = jnp.exp(l0 - mx), jnp.exp(l1 - mx), jnp.exp(l2 - mx)
    ob = (e0 * whole(o_scr, 0) + e1 * whole(o_scr, 1) + e2 * whole(o_scr, 2)) / (e0 + e1 + e2)
    ya = _dot(oa_ref[0], woa_ref[...])
    yb = _dot(ob.astype(BF16), wob_ref[...])
    mrg = ga_ref[0].astype(F32) * ya + gb_ref[0].astype(F32) * yb
    x2 = x_ref[0] + _dot(mrg.astype(BF16), wout_ref[...])
    x2_ref[0] = x2
    ms = jnp.mean(x2 * x2, axis=-1, keepdims=True)
    h2 = x2 * lax.rsqrt(ms + NORM_EPS) * g2_ref[...]
    h2_ref[0] = h2.astype(BF16)
    hh, hl = _split_bf16(h2)
    wh, wl = wrh_ref[...], wrl_ref[...]
    logits = _nt_dot(wh, hh) + _nt_dot(wh, hl) + _nt_dot(wl, hh)
    mxl = jnp.max(logits, axis=0, keepdims=True)
    ex = jnp.exp(logits - mxl)
    aff_ref[0] = ex / jnp.sum(ex, axis=0, keepdims=True)


def _merge(x, oa, obs, lses, gate_a, gate_b, w_o_a, w_o_b, w_out, norm2_g, w_router, tm):
    bsz, seq, d_model = x.shape
    row = lambda w: pl.BlockSpec((1, tm, w), lambda b, i: (b, i, 0))
    full = lambda a: pl.BlockSpec(a.shape, lambda b, i: (0,) * a.ndim)
    res = [pl.BlockSpec((1, dil, tm // dil, B_OUT_W), lambda b, i: (b, 0, i, 0)) for _, dil in B_GROUPS]
    woa, wob, wout = w_o_a.astype(BF16), w_o_b.astype(BF16), w_out.astype(BF16)
    g2 = norm2_g[None, :]
    wrh, wrl = _split_bf16(w_router.T)
    return pl.pallas_call(
        _merge_body,
        grid=(bsz, seq // tm),
        in_specs=[row(d_model), row(A_Q_W)] + res + res + [row(d_model)] * 2
                 + [full(woa), full(wob), full(wout), full(g2), full(wrh), full(wrl)],
        out_specs=[row(d_model), row(d_model),
                   pl.BlockSpec((1, N_EXPERTS, tm), lambda b, i: (b, 0, i))],
        out_shape=[jax.ShapeDtypeStruct((bsz, seq, d_model), F32),
                   jax.ShapeDtypeStruct((bsz, seq, d_model), BF16),
                   jax.ShapeDtypeStruct((bsz, N_EXPERTS, seq), F32)],
        scratch_shapes=[pltpu.VMEM((len(B_GROUPS) * B_OUT_W // LANES, tm, LANES), F32)] * 2,
        compiler_params=pltpu.CompilerParams(
            dimension_semantics=("parallel", "parallel"), vmem_limit_bytes=VMEM_LIMIT),
        name="merge",
    )(x, oa, *obs, *lses, gate_a, gate_b, woa, wob, wout, g2, wrh, wrl)


TOK_BLK = 256
GATHER_ROWS = 128
META_LO = 64
META_HI = 96


def _route_body(aff_ref, pos_ref, post_ref, gt_ref, meta_ref, *, cap):
    aff = aff_ref[0]
    n_e, seq = aff.shape
    bits = pltpu.bitcast(aff, jnp.int32)

    def refine(it, thr):
        cand = thr | jnp.left_shift(jnp.int32(1), 30 - it)
        cnt = jnp.sum(jnp.where(bits >= cand, 1.0, 0.0), axis=1, keepdims=True)
        return jnp.where(cnt >= cap, cand, thr)

    thr = lax.fori_loop(0, 31, refine, jnp.zeros((n_e, 1), jnp.int32))
    above = bits > thr
    tied = bits == thr
    need = cap - jnp.sum(jnp.where(above, 1.0, 0.0), axis=1, keepdims=True)

    ri = lax.broadcasted_iota(jnp.int32, (TOK_BLK, TOK_BLK), 0)
    ci = lax.broadcasted_iota(jnp.int32, (TOK_BLK, TOK_BLK), 1)
    before = jnp.where(ri < ci, 1.0, 0.0).astype(BF16)
    lane = lax.broadcasted_iota(jnp.int32, (1, LANES), 1)
    pad_rows = LANES - n_e

    run_tied = jnp.zeros((n_e, 1), F32)
    run_sel = jnp.zeros((n_e, 1), F32)
    starts = jnp.zeros((n_e, LANES), F32)
    ends = jnp.zeros((n_e, LANES), F32)
    n_blk = seq // TOK_BLK
    for c in range(n_blk):
        sl = slice(c * TOK_BLK, (c + 1) * TOK_BLK)
        tied_c = jnp.where(tied[:, sl], 1.0, 0.0)
        rank = run_tied + _dot(tied_c.astype(BF16), before)
        sel = above[:, sl] | (tied[:, sl] & (rank < need))
        sel_c = jnp.where(sel, 1.0, 0.0)
        pos = run_sel + _dot(sel_c.astype(BF16), before)
        pos_m = jnp.where(sel, pos, -1.0)
        g_m = jnp.where(sel, aff[:, sl], 0.0)
        pos_ref[0, :, sl] = pos_m
        fill = jnp.full((pad_rows, TOK_BLK), -1.0, F32)
        post_ref[0, sl, :] = jnp.concatenate([pos_m, fill], axis=0).T
        gt_ref[0, sl, :] = jnp.concatenate([g_m, jnp.zeros((pad_rows, TOK_BLK), F32)], axis=0).T
        starts = jnp.where(lane == c, run_sel, starts)
        run_tied = run_tied + jnp.sum(tied_c, axis=1, keepdims=True)
        run_sel = run_sel + jnp.sum(sel_c, axis=1, keepdims=True)
        ends = jnp.where(lane == c, run_sel, ends)
    meta = jnp.where(lane == n_blk, run_sel, starts)
    real = lane < n_blk
    for j in range(cap // GATHER_ROWS):
        c0 = float(j * GATHER_ROWS)
        first = jnp.sum(jnp.where(real & (ends <= c0), 1.0, 0.0), axis=1, keepdims=True)
        stop = n_blk - jnp.sum(jnp.where(real & (starts >= c0 + GATHER_ROWS), 1.0, 0.0), axis=1, keepdims=True)
        meta = jnp.where(lane == META_LO + j, first, meta)
        meta = jnp.where(lane == META_HI + j, stop, meta)
    meta_ref[0] = meta.astype(jnp.int32)


def _route(aff, cap):
    bsz, n_e, seq = aff.shape
    return pl.pallas_call(
        functools.partial(_route_body, cap=cap),
        grid=(bsz,),
        in_specs=[pl.BlockSpec((1, n_e, seq), lambda b: (b, 0, 0))],
        out_specs=[pl.BlockSpec((1, n_e, seq), lambda b: (b, 0, 0)),
                   pl.BlockSpec((1, seq, LANES), lambda b: (b, 0, 0)),
                   pl.BlockSpec((1, seq, LANES), lambda b: (b, 0, 0)),
                   pl.BlockSpec((1, n_e, LANES), lambda b: (b, 0, 0))],
        out_shape=[jax.ShapeDtypeStruct((bsz, n_e, seq), F32),
                   jax.ShapeDtypeStruct((bsz, seq, LANES), F32),
                   jax.ShapeDtypeStruct((bsz, seq, LANES), F32),
                   jax.ShapeDtypeStruct((bsz, n_e, LANES), jnp.int32)],
        compiler_params=pltpu.CompilerParams(
            dimension_semantics=("parallel",), vmem_limit_bytes=VMEM_LIMIT),
        name="route",
    )(aff)


GATHER_SPAN = 4


def _gather_body(meta_ref, h_ref, pos_ref, xin_ref, acc_ref, *, cap):
    b, e = pl.program_id(0), pl.program_id(1)
    base = (b * N_EXPERTS + e) * LANES
    n_blk = pos_ref.shape[2]
    half = GATHER_SPAN // 2
    slot = lax.broadcasted_iota(jnp.int32, (GATHER_ROWS, 1), 0)
    for j in range(cap // GATHER_ROWS):
        want = (slot + j * GATHER_ROWS).astype(F32)
        acc_ref[...] = jnp.zeros_like(acc_ref)
        first = meta_ref[base + META_LO + j]
        stop = meta_ref[base + META_HI + j]

        def add_blocks(i, carry, want=want, first=first):
            begin = first + GATHER_SPAN * i
            begin_c = jnp.minimum(begin, n_blk - GATHER_SPAN)
            for c in range(2):
                blk0 = begin_c + c * half
                pieces = []
                for t in range(half):
                    pos = pos_ref[0, 0, pl.ds(blk0 + t, 1), :]
                    pos = jnp.where(blk0 + t >= begin, pos, -1.0)
                    pieces.append(jnp.where(pos == want, 1.0, 0.0))
                onehot = jnp.concatenate(pieces, axis=1).astype(BF16)
                hblk = h_ref[0, pl.ds(pl.multiple_of(blk0 * TOK_BLK, TOK_BLK), half * TOK_BLK), :]
                acc_ref[c] += _dot(onehot, hblk)
            return carry

        lax.fori_loop(0, (stop - first + GATHER_SPAN - 1) // GATHER_SPAN, add_blocks, 0)
        xin_ref[0, 0, j * GATHER_ROWS:(j + 1) * GATHER_ROWS, :] = (acc_ref[0] + acc_ref[1]).astype(xin_ref.dtype)


def _gather(meta_flat, h2, pos, cap):
    bsz, seq, d_model = h2.shape
    n_e = pos.shape[1]
    assert seq // TOK_BLK >= GATHER_SPAN
    pos4 = pos.reshape(bsz, n_e, seq // TOK_BLK, TOK_BLK)
    return pl.pallas_call(
        functools.partial(_gather_body, cap=cap),
        grid_spec=pltpu.PrefetchScalarGridSpec(
            num_scalar_prefetch=1,
            grid=(bsz, n_e),
            in_specs=[pl.BlockSpec((1, seq, d_model), lambda b, e, m: (b, 0, 0)),
                      pl.BlockSpec((1, 1, seq // TOK_BLK, TOK_BLK), lambda b, e, m: (b, e, 0, 0))],
            out_specs=pl.BlockSpec((1, 1, cap, d_model), lambda b, e, m: (b, e, 0, 0)),
            scratch_shapes=[pltpu.VMEM((2, GATHER_ROWS, d_model), F32)]),
        out_shape=jax.ShapeDtypeStruct((bsz, n_e, cap, d_model), BF16),
        compiler_params=pltpu.CompilerParams(
            dimension_semantics=("parallel", "parallel"), vmem_limit_bytes=VMEM_LIMIT),
        name="gather",
    )(meta_flat, h2, pos4)


FFN_ROWS = 256


def _ffn_body(x_ref, wg_ref, wu_ref, wd_ref, y_ref):
    cap = x_ref.shape[2]
    for r in range(cap // FFN_ROWS):
        rows = slice(r * FFN_ROWS, (r + 1) * FFN_ROWS)
        xin = x_ref[0, 0, rows, :]
        a = _dot(xin, wg_ref[0])
        u = _dot(xin, wu_ref[0])
        mid = (a * jax.nn.sigmoid(a) * u).astype(BF16)
        y_ref[0, 0, rows, :] = _dot(mid, wd_ref[0]).astype(y_ref.dtype)


def _ffn(xin, w_gate, w_up, w_down):
    bsz, n_e, cap, d_model = xin.shape
    ff = w_gate.shape[2]
    tok = pl.BlockSpec((1, 1, cap, d_model), lambda e, b: (b, e, 0, 0))
    return pl.pallas_call(
        _ffn_body,
        grid=(n_e, bsz),
        in_specs=[tok,
                  pl.BlockSpec((1, d_model, ff), lambda e, b: (e, 0, 0)),
                  pl.BlockSpec((1, d_model, ff), lambda e, b: (e, 0, 0)),
                  pl.BlockSpec((1, ff, d_model), lambda e, b: (e, 0, 0))],
        out_specs=tok,
        out_shape=jax.ShapeDtypeStruct((bsz, n_e, cap, d_model), BF16),
        compiler_params=pltpu.CompilerParams(
            dimension_semantics=("parallel", "parallel"), vmem_limit_bytes=VMEM_LIMIT),
        name="ffn",
    )(xin, w_gate.astype(BF16), w_up.astype(BF16), w_down.astype(BF16))


SCATTER_ROWS = 256


def _scatter_body(meta_ref, y_ref, post_ref, gt_ref, x2_ref, gf_ref, o_ref, *, tc):
    b, t, e = pl.program_id(0), pl.program_id(1), pl.program_id(2)
    base = (b * N_EXPERTS + e) * LANES

    @pl.when(e == 0)
    def _():
        o_ref[0] = x2_ref[0]

    lane = lax.broadcasted_iota(jnp.int32, (1, LANES), 1)
    mine = lane == e
    slot = lax.broadcasted_iota(jnp.int32, (1, SCATTER_ROWS), 1)
    for kk in range(tc // TOK_BLK):
        rows = slice(kk * TOK_BLK, (kk + 1) * TOK_BLK)
        blk = t * (tc // TOK_BLK) + kk
        first = meta_ref[base + blk]
        last = meta_ref[base + blk + 1]
        tile0 = first // SCATTER_ROWS

        def add_tile(j, rows=rows):
            pcol = jnp.sum(jnp.where(mine, post_ref[0, rows, :], 0.0), axis=1, keepdims=True)
            gcol = jnp.sum(jnp.where(mine, gt_ref[0, rows, :], 0.0), axis=1, keepdims=True)
            want = (slot + j * SCATTER_ROWS).astype(F32)
            onehot = jnp.where(pcol == want, 1.0, 0.0).astype(BF16)
            ytile = y_ref[0, 0, pl.ds(pl.multiple_of(j * SCATTER_ROWS, SCATTER_ROWS), SCATTER_ROWS), :]
            o_ref[0, rows, :] += gcol * _dot(onehot, ytile)

        @pl.when(last > first)
        def _():
            add_tile(tile0)

        @pl.when(last > (tile0 + 1) * SCATTER_ROWS)
        def _():
            add_tile(tile0 + 1)

    @pl.when(e == N_EXPERTS - 1)
    def _():
        v = o_ref[0]
        ms = jnp.mean(v * v, axis=-1, keepdims=True)
        o_ref[0] = v * lax.rsqrt(ms + NORM_EPS) * gf_ref[...]


def _scatter(meta_flat, y, post, gt, x2, normf_g, tc):
    bsz, n_e, cap, d_model = y.shape
    seq = x2.shape[1]
    tc = min(tc, seq)
    gf = normf_g[None, :]
    tokw = lambda w: pl.BlockSpec((1, tc, w), lambda b, t, e, m: (b, t, 0))
    return pl.pallas_call(
        functools.partial(_scatter_body, tc=tc),
        grid_spec=pltpu.PrefetchScalarGridSpec(
            num_scalar_prefetch=1,
            grid=(bsz, seq // tc, n_e),
            in_specs=[pl.BlockSpec((1, 1, cap, d_model), lambda b, t, e, m: (b, e, 0, 0)),
                      tokw(LANES), tokw(LANES), tokw(d_model),
                      pl.BlockSpec(gf.shape, lambda b, t, e, m: (0, 0))],
            out_specs=tokw(d_model)),
        out_shape=jax.ShapeDtypeStruct((bsz, seq, d_model), F32),
        compiler_params=pltpu.CompilerParams(
            dimension_semantics=("parallel", "parallel", "arbitrary"), vmem_limit_bytes=VMEM_LIMIT),
        name="scatter",
    )(meta_flat, y, post, gt, x2, gf)


def kernel(x, norm1_g, w_in, b_gates, q_norm_g, k_norm_g, w_o_a, w_o_b, w_out, norm2_g, w_router,
           w_gate, w_up, w_down, normf_g):
    bsz, seq, d_model = x.shape
    depth = norm1_g.shape[0]
    cap = EC_CAPACITY_FACTOR * seq // N_EXPERTS
    tm = min(512, seq)
    for l in range(depth):
        qa, ka, va, gate_a, gate_b, qkv_b = _in_proj(
            x, norm1_g[l], w_in[l], b_gates[l], q_norm_g[l], k_norm_g[l], tm)
        oa = _attn_a(qa, ka, va, tq=min(128, seq), tk=min(512, seq))
        obs, lses = [], []
        for gi, (_, dil) in enumerate(B_GROUPS):
            o, lse = _attn_b_group(*qkv_b[gi], gi, dil, tu=512)
            obs.append(o)
            lses.append(lse)
        x2, h2, aff = _merge(x, oa, obs, lses, gate_a, gate_b, w_o_a[l], w_o_b[l], w_out[l],
                             norm2_g[l], w_router[l], tm)
        pos, post, gt, meta = _route(aff, cap)
        meta_flat = meta.reshape(-1)
        xin = _gather(meta_flat, h2, pos, cap)
        y = _ffn(xin, w_gate[l], w_up[l], w_down[l])
        assert depth == 1
        x = _scatter(meta_flat, y, post, gt, x2, normf_g, tc=2048)
    return x
```

```python
import functools

import numpy as np
import jax
import jax.numpy as jnp
from jax import lax
from jax.experimental import pallas as pl
from jax.experimental.pallas import tpu as pltpu

HEAD_DIM = 64
A_Q_HEADS = 8
A_KV_HEADS = 2
A_GROUP = A_Q_HEADS // A_KV_HEADS
B_GROUPS = ((128, 1), (512, 4), (2048, 16))
B_HEADS_PER_GROUP = 4
B_HEADS = B_HEADS_PER_GROUP * len(B_GROUPS)
GRID_W = 64
ROPE_THETA = 10000.0
N_EXPERTS = 16
EC_CAPACITY_FACTOR = 2
NORM_EPS = 1e-6
NEG_INF = -1e30

A_Q_W = A_Q_HEADS * HEAD_DIM
A_KV_W = A_KV_HEADS * HEAD_DIM
B_W = B_HEADS * HEAD_DIM
B_OUT_W = B_HEADS_PER_GROUP * HEAD_DIM

LANES = 128
MXU_DIM = 256
VMEM_LIMIT = 56 * 1024 * 1024

F32 = jnp.float32
BF16 = jnp.bfloat16
LOG2_E = 1.4426950408889634


def _nt_dot(a, b):
    return lax.dot_general(a, b, (((1,), (1,)), ((), ())), preferred_element_type=F32)


def _dot(a, b):
    return jnp.dot(a, b, preferred_element_type=F32)


def _split_bf16(v):
    hi = v.astype(BF16)
    lo = (v - hi.astype(F32)).astype(BF16)
    return hi, lo


def _in_proj_body(x_ref, g1_ref, w_ref, bg_ref, qg_ref, kg_ref, cos_ref, sin_ref, hs_ref,
                  qa_ref, ka_ref, va_ref, ga_ref, gb_ref, *rest, d_model):
    b_refs, zb_scr = rest[:-1], rest[-1]
    tm = x_ref.shape[1]
    x = x_ref[0]
    ms = jnp.mean(x * x, axis=-1, keepdims=True)
    h = (x * lax.rsqrt(ms + NORM_EPS) * g1_ref[...]).astype(BF16)

    def proj(lo, width):
        return _dot(h, w_ref[:, lo:lo + width])

    cos = cos_ref[...]
    sin = sin_ref[...]
    lane = lax.broadcasted_iota(jnp.int32, (1, LANES), 1)
    low = (lane % 32) < 16

    def head_norm_rope(z, gain, nrep):
        width = LANES * nrep
        z2 = z * z
        hi, lo = _split_bf16(z2)
        hs = hs_ref[:width, :width]
        ss = _dot(hi, hs) + _dot(lo, hs)
        zn = z * lax.rsqrt(ss * (1.0 / HEAD_DIM) + NORM_EPS) * gain
        rep = lambda t: jnp.concatenate([t] * nrep, axis=1) if nrep > 1 else t
        partner = jnp.where(rep(low), pltpu.roll(zn, width - 16, 1), pltpu.roll(zn, 16, 1))
        return zn * rep(cos) + partner * rep(sin)

    scale = HEAD_DIM ** -0.5
    za = proj(0, A_Q_W + 2 * A_KV_W)
    qa = head_norm_rope(za[:, :A_Q_W], qg_ref[...], A_Q_W // LANES) * (scale * LOG2_E)
    ka = head_norm_rope(za[:, A_Q_W:A_Q_W + A_KV_W], kg_ref[...], A_KV_W // LANES)
    qa_ref[0] = qa.astype(BF16)
    ka_ref[0] = ka.astype(BF16)
    va = za[:, A_Q_W + A_KV_W:]
    va_ref[0] = jnp.concatenate([va, jnp.ones_like(va)], axis=1).astype(BF16)
    off = A_Q_W + 2 * A_KV_W
    slabs = B_OUT_W // LANES
    for t in range(3):
        z = proj(off + t * B_W, B_W) * (scale if t == 0 else 1.0)
        for c in range(B_W // LANES):
            zb_scr[c] = z[:, c * LANES:(c + 1) * LANES]
        for gi, (_, dil) in enumerate(B_GROUPS):
            out = b_refs[3 * gi + t]
            for r in range(dil):
                for c in range(slabs):
                    out[0, r, :, c * LANES:(c + 1) * LANES] = (
                        zb_scr[gi * slabs + c, pl.ds(r, tm // dil, stride=dil), :].astype(BF16))
    off += 3 * B_W
    ga_ref[0] = jax.nn.sigmoid(proj(off, d_model) + bg_ref[:, :d_model]).astype(BF16)
    gb_ref[0] = jax.nn.sigmoid(proj(off + d_model, d_model) + bg_ref[:, d_model:]).astype(BF16)


def _rope_tables(seq):
    rows = seq // GRID_W
    row_id = jnp.repeat(jnp.arange(rows, dtype=F32), GRID_W)
    col_id = jnp.tile(jnp.arange(GRID_W, dtype=F32), rows)
    half = HEAD_DIM // 2
    inv_freq = 1.0 / (ROPE_THETA ** (jnp.arange(0, half, 2, dtype=F32) / half))
    ang_r = row_id[:, None] * inv_freq[None, :]
    ang_c = col_id[:, None] * inv_freq[None, :]
    cos = jnp.concatenate([jnp.cos(ang_r)] * 2 + [jnp.cos(ang_c)] * 2, axis=1)
    sin = jnp.concatenate([-jnp.sin(ang_r), jnp.sin(ang_r), -jnp.sin(ang_c), jnp.sin(ang_c)], axis=1)
    return jnp.tile(cos, (1, 2)), jnp.tile(sin, (1, 2))


def _in_proj(x, norm1_g, w_in, b_gates, q_norm_g, k_norm_g, tm):
    bsz, seq, d_model = x.shape
    cos, sin = _rope_tables(seq)
    head_id = np.arange(A_Q_W) // HEAD_DIM
    hs = jnp.asarray(head_id[:, None] == head_id[None, :], BF16)
    qg = jnp.tile(q_norm_g, A_Q_W // HEAD_DIM)[None, :]
    kg = jnp.tile(k_norm_g, A_KV_W // HEAD_DIM)[None, :]
    row = lambda w: pl.BlockSpec((1, tm, w), lambda b, i: (b, i, 0))
    full = lambda a: pl.BlockSpec(a.shape, lambda b, i: (0,) * a.ndim)
    tab = pl.BlockSpec((tm, LANES), lambda b, i: (i, 0))
    g1 = norm1_g[None, :]
    bg = b_gates[None, :]
    w = w_in.astype(BF16)
    shp = lambda wd: jax.ShapeDtypeStruct((bsz, seq, wd), BF16)
    b_specs, b_shapes = [], []
    for _, dil in B_GROUPS:
        assert tm % (16 * dil) == 0
        b_specs += [pl.BlockSpec((1, dil, tm // dil, B_OUT_W), lambda b, i: (b, 0, i, 0))] * 3
        b_shapes += [jax.ShapeDtypeStruct((bsz, dil, seq // dil, B_OUT_W), BF16)] * 3
    outs = pl.pallas_call(
        functools.partial(_in_proj_body, d_model=d_model),
        grid=(bsz, seq // tm),
        in_specs=[row(d_model), full(g1), full(w), full(bg), full(qg), full(kg), tab, tab, full(hs)],
        out_specs=[row(A_Q_W), row(A_KV_W), row(2 * A_KV_W), row(d_model), row(d_model)] + b_specs,
        out_shape=[shp(A_Q_W), shp(A_KV_W), shp(2 * A_KV_W), shp(d_model), shp(d_model)] + b_shapes,
        scratch_shapes=[pltpu.VMEM((B_W // LANES, tm, LANES), F32)],
        compiler_params=pltpu.CompilerParams(
            dimension_semantics=("parallel", "parallel"), vmem_limit_bytes=VMEM_LIMIT),
        name="in_proj",
    )(x, g1, w, bg, qg, kg, cos, sin, hs)
    qa, ka, va, gate_a, gate_b = outs[:5]
    qkv_b = [outs[5 + 3 * gi:8 + 3 * gi] for gi in range(len(B_GROUPS))]
    return qa, ka, va, gate_a, gate_b, qkv_b


def _attn_a_body(q_ref, k_ref, v_ref, o_ref, qp_scr, s_scr, p_scr, m_scr, a_scr, acc_scr, *, tk):
    tq = q_ref.shape[1]
    seq = k_ref.shape[1]
    nk = seq // tk
    rows_all = A_Q_HEADS * tq
    q = q_ref[0].astype(F32)
    zeros = jnp.zeros((tq, HEAD_DIM), F32)
    for g in range(A_Q_HEADS):
        qg = q[:, g * HEAD_DIM:(g + 1) * HEAD_DIM]
        qp = jnp.concatenate([qg, zeros] if g // A_GROUP == 0 else [zeros, qg], axis=1)
        qp_scr[g * tq:(g + 1) * tq, :] = qp.astype(BF16)
    m_scr[...] = jnp.full_like(m_scr, NEG_INF)
    acc_scr[...] = jnp.zeros_like(acc_scr)

    def scores(kt, slot):
        start = pl.multiple_of(kt * tk, tk)
        s_scr[slot] = _nt_dot(qp_scr[...], k_ref[0, pl.ds(start, tk), :])

    def softmax_pv(kt, slot):
        for c in range(rows_all // ATTN_A_CHUNK):
            rows = slice(c * ATTN_A_CHUNK, (c + 1) * ATTN_A_CHUNK)
            s = s_scr[slot, rows, :]
            m_prev = m_scr[rows, :]
            m_new = jnp.maximum(m_prev, jnp.max(s, axis=1, keepdims=True))
            m_scr[rows, :] = m_new
            a_scr[rows, :] = jnp.exp2(m_prev - m_new)
            d = s - jnp.concatenate([m_new] * (tk // LANES), axis=1)
            p_scr[rows, :] = jnp.exp2(d.astype(BF16))
        start = pl.multiple_of(kt * tk, tk)
        pv = _dot(p_scr[...], v_ref[0, pl.ds(start, tk), :])
        alpha = a_scr[...]
        acc_scr[...] = jnp.concatenate([alpha, alpha], axis=1) * acc_scr[...] + pv

    scores(0, 0)

    def trip(i, carry):
        for u in range(ATTN_A_UNROLL):
            kt = ATTN_A_UNROLL * i + u
            scores(kt + 1, (u + 1) % 2)
            softmax_pv(kt, u % 2)
        return carry

    lax.fori_loop(0, nk // ATTN_A_UNROLL - 1, trip, 0)
    for kt in range(nk - ATTN_A_UNROLL, nk):
        if kt + 1 < nk:
            scores(kt + 1, (kt + 1) % 2)
        softmax_pv(kt, kt % 2)

    outs = []
    for g in range(A_Q_HEADS):
        kv = g // A_GROUP
        rows = slice(g * tq, (g + 1) * tq)
        outs.append(acc_scr[rows, kv * HEAD_DIM:(kv + 1) * HEAD_DIM]
                    / acc_scr[rows, A_KV_W:A_KV_W + HEAD_DIM])
    o_ref[0] = jnp.concatenate(outs, axis=1).astype(o_ref.dtype)


ATTN_A_CHUNK = 128
ATTN_A_UNROLL = 4


def _attn_a(qa, ka, va, tq, tk):
    bsz, seq, _ = qa.shape
    assert (seq // tk) % ATTN_A_UNROLL == 0
    rows_all = A_Q_HEADS * tq
    return pl.pallas_call(
        functools.partial(_attn_a_body, tk=tk),
        grid=(bsz, seq // tq),
        in_specs=[pl.BlockSpec((1, tq, A_Q_W), lambda b, i: (b, i, 0)),
                  pl.BlockSpec((1, seq, A_KV_W), lambda b, i: (b, 0, 0)),
                  pl.BlockSpec((1, seq, 2 * A_KV_W), lambda b, i: (b, 0, 0))],
        out_specs=pl.BlockSpec((1, tq, A_Q_W), lambda b, i: (b, i, 0)),
        out_shape=jax.ShapeDtypeStruct((bsz, seq, A_Q_W), BF16),
        scratch_shapes=[pltpu.VMEM((rows_all, A_KV_W), BF16),
                        pltpu.VMEM((2, rows_all, tk), F32),
                        pltpu.VMEM((rows_all, tk), BF16),
                        pltpu.VMEM((rows_all, LANES), F32),
                        pltpu.VMEM((rows_all, LANES), F32),
                        pltpu.VMEM((rows_all, 2 * A_KV_W), F32)],
        compiler_params=pltpu.CompilerParams(
            dimension_semantics=("parallel", "parallel"), vmem_limit_bytes=VMEM_LIMIT),
        name="attn_a",
    )(qa, ka, va)


HALO = 128
SUB = 128
N_SIDE = 64
WIN = SUB + 2 * N_SIDE


def _attn_b_body(q_ref, kp_ref, kc_ref, kn_ref, vp_ref, vc_ref, vn_ref, bias_ref, o_ref, lse_ref,
                 kbuf, vbuf, *, seq_d):
    tu = q_ref.shape[2]
    i = pl.program_id(2)
    kbuf[0:HALO] = kp_ref[0, 0]
    kbuf[HALO:HALO + tu] = kc_ref[0, 0]
    kbuf[HALO + tu:] = kn_ref[0, 0]
    vbuf[0:HALO] = vp_ref[0, 0]
    vbuf[HALO:HALO + tu] = vc_ref[0, 0]
    vbuf[HALO + tu:] = vn_ref[0, 0]

    nh = B_HEADS_PER_GROUP
    nsub = tu // SUB
    head_of_lane = lax.broadcasted_iota(jnp.int32, (1, B_OUT_W), 1) // HEAD_DIM
    col = lax.broadcasted_iota(jnp.int32, (1, WIN), 1)

    def scores(j):
        qs = q_ref[0, 0, j * SUB:(j + 1) * SUB, :]
        qst = jnp.concatenate([jnp.where(head_of_lane == hh, qs, jnp.zeros_like(qs)) for hh in range(nh)],
                              axis=0)
        w0 = HALO - N_SIDE + j * SUB
        return _nt_dot(qst, kbuf[w0:w0 + WIN, :])

    s_next = scores(0)
    for j in range(nsub):
        s = s_next + bias_ref[...]
        if j + 1 < nsub:
            s_next = scores(j + 1)
        if j == 0 or j == nsub - 1:
            key_u = i * tu + j * SUB - N_SIDE + col
            s = jnp.where((key_u >= 0) & (key_u < seq_d), s, NEG_INF)
        m = jnp.max(s, axis=1, keepdims=True)
        p = jnp.exp(s - m)
        l = jnp.sum(p, axis=1, keepdims=True)
        w0 = HALO - N_SIDE + j * SUB
        pv = _dot(p.astype(BF16), vbuf[w0:w0 + WIN, :])
        on = pv / l
        lse = m + jnp.log(l)
        o_acc = jnp.zeros((SUB, B_OUT_W), F32)
        lse_acc = jnp.zeros((SUB, B_OUT_W), F32)
        for hh in range(nh):
            rows = slice(hh * SUB, (hh + 1) * SUB)
            mine = head_of_lane == hh
            o_acc = jnp.where(mine, on[rows], o_acc)
            lse_acc = jnp.where(mine, lse[rows], lse_acc)
        o_ref[0, 0, j * SUB:(j + 1) * SUB, :] = o_acc
        lse_ref[0, 0, j * SUB:(j + 1) * SUB, :] = lse_acc


def _band_bias(group, dil):
    row = np.arange(SUB)[:, None]
    col = np.arange(WIN)[None, :]
    rel = col - N_SIDE - row
    tiles = []
    for hh in range(B_HEADS_PER_GROUP):
        slope = 2.0 ** (-8.0 * (group * B_HEADS_PER_GROUP + hh + 1) / B_HEADS)
        tiles.append(np.where(np.abs(rel) <= N_SIDE, -slope * np.abs(rel) * dil, NEG_INF))
    return jnp.asarray(np.concatenate(tiles, axis=0), F32)


def _attn_b_group(q, k, v, group, dil, tu):
    bsz, _, seq_d, _ = q.shape
    tu = min(tu, seq_d)
    per = tu // HALO
    last = seq_d // HALO - 1
    bias = _band_bias(group, dil)
    cur = pl.BlockSpec((1, 1, tu, B_OUT_W), lambda b, r, i: (b, r, i, 0))
    prev = pl.BlockSpec((1, 1, HALO, B_OUT_W), lambda b, r, i: (b, r, jnp.maximum(i * per - 1, 0), 0))
    nxt = pl.BlockSpec((1, 1, HALO, B_OUT_W), lambda b, r, i: (b, r, jnp.minimum((i + 1) * per, last), 0))
    return pl.pallas_call(
        functools.partial(_attn_b_body, seq_d=seq_d),
        grid=(bsz, dil, seq_d // tu),
        in_specs=[cur, prev, cur, nxt, prev, cur, nxt,
                  pl.BlockSpec(bias.shape, lambda b, r, i: (0, 0))],
        out_specs=[cur, cur],
        out_shape=[jax.ShapeDtypeStruct((bsz, dil, seq_d, B_OUT_W), F32)] * 2,
        scratch_shapes=[pltpu.VMEM((tu + 2 * HALO, B_OUT_W), BF16)] * 2,
        compiler_params=pltpu.CompilerParams(
            dimension_semantics=("parallel", "parallel", "parallel"), vmem_limit_bytes=VMEM_LIMIT),
        name=f"attn_b{group}",
    )(q, k, k, k, v, v, v, bias)


def _merge_body(x_ref, oa_ref, o0_ref, o1_ref, o2_ref, l0_ref, l1_ref, l2_ref, ga_ref, gb_ref,
                woa_ref, wob_ref, wout_ref, g2_ref, wrh_ref, wrl_ref,
                x2_ref, h2_ref, aff_ref, o_scr, l_scr):
    tm = x_ref.shape[1]
    slabs = B_OUT_W // LANES
    for gi, (o_ref, lse_ref) in enumerate(((o0_ref, l0_ref), (o1_ref, l1_ref), (o2_ref, l2_ref))):
        dil = B_GROUPS[gi][1]
        for r in range(dil):
            for c in range(slabs):
                lanes = slice(c * LANES, (c + 1) * LANES)
                o_scr[gi * slabs + c, pl.ds(r, tm // dil, stride=dil), :] = o_ref[0, r, :, lanes]
                l_scr[gi * slabs + c, pl.ds(r, tm // dil, stride=dil), :] = lse_ref[0, r, :, lanes]
    whole = lambda scr, gi: jnp.concatenate([scr[gi * slabs + c] for c in range(slabs)], axis=1)
    l0, l1, l2 = whole(l_scr, 0), whole(l_scr, 1), whole(l_scr, 2)
    mx = jnp.maximum(jnp.maximum(l0, l1), l2)
    e0, e1, e2 = jnp.exp(l0 - mx), jnp.exp(l1 - mx), jnp.exp(l2 - mx)
    ob = (e0 * whole(o_scr, 0) + e1 * whole(o_scr, 1) + e2 * whole(o_scr, 2)) / (e0 + e1 + e2)
    ya = _dot(oa_ref[0], woa_ref[...])
    yb = _dot(ob.astype(BF16), wob_ref[...])
    mrg = ga_ref[0].astype(F32) * ya + gb_ref[0].astype(F32) * yb
    x2 = x_ref[0] + _dot(mrg.astype(BF16), wout_ref[...])
    x2_ref[0] = x2
    ms = jnp.mean(x2 * x2, axis=-1, keepdims=True)
    h2 = x2 * lax.rsqrt(ms + NORM_EPS) * g2_ref[...]
    h2_ref[0] = h2.astype(BF16)
    hh, hl = _split_bf16(h2)
    wh, wl = wrh_ref[...], wrl_ref[...]
    logits = _nt_dot(wh, hh) + _nt_dot(wh, hl) + _nt_dot(wl, hh)
    mxl = jnp.max(logits, axis=0, keepdims=True)
    ex = jnp.exp(logits - mxl)
    aff_ref[0] = ex / jnp.sum(ex, axis=0, keepdims=True)


def _merge(x, oa, obs, lses, gate_a, gate_b, w_o_a, w_o_b, w_out, norm2_g, w_router, tm):
    bsz, seq, d_model = x.shape
    row = lambda w: pl.BlockSpec((1, tm, w), lambda b, i: (b, i, 0))
    full = lambda a: pl.BlockSpec(a.shape, lambda b, i: (0,) * a.ndim)
    res = [pl.BlockSpec((1, dil, tm // dil, B_OUT_W), lambda b, i: (b, 0, i, 0)) for _, dil in B_GROUPS]
    woa, wob, wout = w_o_a.astype(BF16), w_o_b.astype(BF16), w_out.astype(BF16)
    g2 = norm2_g[None, :]
    wrh, wrl = _split_bf16(w_router.T)
    return pl.pallas_call(
        _merge_body,
        grid=(bsz, seq // tm),
        in_specs=[row(d_model), row(A_Q_W)] + res + res + [row(d_model)] * 2
                 + [full(woa), full(wob), full(wout), full(g2), full(wrh), full(wrl)],
        out_specs=[row(d_model), row(d_model),
                   pl.BlockSpec((1, N_EXPERTS, tm), lambda b, i: (b, 0, i))],
        out_shape=[jax.ShapeDtypeStruct((bsz, seq, d_model), F32),
                   jax.ShapeDtypeStruct((bsz, seq, d_model), BF16),
                   jax.ShapeDtypeStruct((bsz, N_EXPERTS, seq), F32)],
        scratch_shapes=[pltpu.VMEM((len(B_GROUPS) * B_OUT_W // LANES, tm, LANES), F32)] * 2,
        compiler_params=pltpu.CompilerParams(
            dimension_semantics=("parallel", "parallel"), vmem_limit_bytes=VMEM_LIMIT),
        name="merge",
    )(x, oa, *obs, *lses, gate_a, gate_b, woa, wob, wout, g2, wrh, wrl)


TOK_BLK = 256
GATHER_ROWS = 128
META_LO = 64
META_HI = 96


def _route_body(aff_ref, pos_ref, post_ref, gt_ref, meta_ref, *, cap):
    aff = aff_ref[0]
    n_e, seq = aff.shape
    bits = pltpu.bitcast(aff, jnp.int32)

    def refine(it, thr):
        cand = thr | jnp.left_shift(jnp.int32(1), 30 - it)
        cnt = jnp.sum(jnp.where(bits >= cand, 1.0, 0.0), axis=1, keepdims=True)
        return jnp.where(cnt >= cap, cand, thr)

    thr = lax.fori_loop(0, 31, refine, jnp.zeros((n_e, 1), jnp.int32))
    above = bits > thr
    tied = bits == thr
    need = cap - jnp.sum(jnp.where(above, 1.0, 0.0), axis=1, keepdims=True)

    ri = lax.broadcasted_iota(jnp.int32, (TOK_BLK, TOK_BLK), 0)
    ci = lax.broadcasted_iota(jnp.int32, (TOK_BLK, TOK_BLK), 1)
    before = jnp.where(ri < ci, 1.0, 0.0).astype(BF16)
    lane = lax.broadcasted_iota(jnp.int32, (1, LANES), 1)
    pad_rows = LANES - n_e

    run_tied = jnp.zeros((n_e, 1), F32)
    run_sel = jnp.zeros((n_e, 1), F32)
    starts = jnp.zeros((n_e, LANES), F32)
    ends = jnp.zeros((n_e, LANES), F32)
    n_blk = seq // TOK_BLK
    for c in range(n_blk):
        sl = slice(c * TOK_BLK, (c + 1) * TOK_BLK)
        tied_c = jnp.where(tied[:, sl], 1.0, 0.0)
        rank = run_tied + _dot(tied_c.astype(BF16), before)
        sel = above[:, sl] | (tied[:, sl] & (rank < need))
        sel_c = jnp.where(sel, 1.0, 0.0)
        pos = run_sel + _dot(sel_c.astype(BF16), before)
        pos_m = jnp.where(sel, pos, -1.0)
        g_m = jnp.where(sel, aff[:, sl], 0.0)
        pos_ref[0, :, sl] = pos_m
        fill = jnp.full((pad_rows, TOK_BLK), -1.0, F32)
        post_ref[0, sl, :] = jnp.concatenate([pos_m, fill], axis=0).T
        gt_ref[0, sl, :] = jnp.concatenate([g_m, jnp.zeros((pad_rows, TOK_BLK), F32)], axis=0).T
        starts = jnp.where(lane == c, run_sel, starts)
        run_tied = run_tied + jnp.sum(tied_c, axis=1, keepdims=True)
        run_sel = run_sel + jnp.sum(sel_c, axis=1, keepdims=True)
        ends = jnp.where(lane == c, run_sel, ends)
    meta = jnp.where(lane == n_blk, run_sel, starts)
    real = lane < n_blk
    for j in range(cap // GATHER_ROWS):
        c0 = float(j * GATHER_ROWS)
        first = jnp.sum(jnp.where(real & (ends <= c0), 1.0, 0.0), axis=1, keepdims=True)
        stop = n_blk - jnp.sum(jnp.where(real & (starts >= c0 + GATHER_ROWS), 1.0, 0.0), axis=1, keepdims=True)
        meta = jnp.where(lane == META_LO + j, first, meta)
        meta = jnp.where(lane == META_HI + j, stop, meta)
    meta_ref[0] = meta.astype(jnp.int32)


def _route(aff, cap):
    bsz, n_e, seq = aff.shape
    return pl.pallas_call(
        functools.partial(_route_body, cap=cap),
        grid=(bsz,),
        in_specs=[pl.BlockSpec((1, n_e, seq), lambda b: (b, 0, 0))],
        out_specs=[pl.BlockSpec((1, n_e, seq), lambda b: (b, 0, 0)),
                   pl.BlockSpec((1, seq, LANES), lambda b: (b, 0, 0)),
                   pl.BlockSpec((1, seq, LANES), lambda b: (b, 0, 0)),
                   pl.BlockSpec((1, n_e, LANES), lambda b: (b, 0, 0))],
        out_shape=[jax.ShapeDtypeStruct((bsz, n_e, seq), F32),
                   jax.ShapeDtypeStruct((bsz, seq, LANES), F32),
                   jax.ShapeDtypeStruct((bsz, seq, LANES), F32),
                   jax.ShapeDtypeStruct((bsz, n_e, LANES), jnp.int32)],
        compiler_params=pltpu.CompilerParams(
            dimension_semantics=("parallel",), vmem_limit_bytes=VMEM_LIMIT),
        name="route",
    )(aff)


GATHER_SPAN = 6


def _gather_body(meta_ref, h_ref, pos_ref, xin_ref, acc_ref, *, cap):
    b, e = pl.program_id(0), pl.program_id(1)
    base = (b * N_EXPERTS + e) * LANES
    n_blk = pos_ref.shape[2]
    half = GATHER_SPAN // 2
    slot = lax.broadcasted_iota(jnp.int32, (GATHER_ROWS, 1), 0)
    for j in range(cap // GATHER_ROWS):
        want = (slot + j * GATHER_ROWS).astype(F32)
        acc_ref[...] = jnp.zeros_like(acc_ref)
        first = meta_ref[base + META_LO + j]
        stop = meta_ref[base + META_HI + j]

        def add_blocks(i, carry, want=want, first=first):
            begin = first + GATHER_SPAN * i
            begin_c = jnp.minimum(begin, n_blk - GATHER_SPAN)
            for c in range(2):
                blk0 = begin_c + c * half
                pieces = []
                for t in range(half):
                    pos = pos_ref[0, 0, pl.ds(blk0 + t, 1), :]
                    pos = jnp.where(blk0 + t >= begin, pos, -1.0)
                    pieces.append(jnp.where(pos == want, 1.0, 0.0))
                onehot = jnp.concatenate(pieces, axis=1).astype(BF16)
                hblk = h_ref[0, pl.ds(pl.multiple_of(blk0 * TOK_BLK, TOK_BLK), half * TOK_BLK), :]
                acc_ref[c] += _dot(onehot, hblk)
            return carry

        lax.fori_loop(0, (stop - first + GATHER_SPAN - 1) // GATHER_SPAN, add_blocks, 0)
        xin_ref[0, 0, j * GATHER_ROWS:(j + 1) * GATHER_ROWS, :] = (acc_ref[0] + acc_ref[1]).astype(xin_ref.dtype)


def _gather(meta_flat, h2, pos, cap):
    bsz, seq, d_model = h2.shape
    n_e = pos.shape[1]
    assert seq // TOK_BLK >= GATHER_SPAN
    pos4 = pos.reshape(bsz, n_e, seq // TOK_BLK, TOK_BLK)
    return pl.pallas_call(
        functools.partial(_gather_body, cap=cap),
        grid_spec=pltpu.PrefetchScalarGridSpec(
            num_scalar_prefetch=1,
            grid=(bsz, n_e),
            in_specs=[pl.BlockSpec((1, seq, d_model), lambda b, e, m: (b, 0, 0)),
                      pl.BlockSpec((1, 1, seq // TOK_BLK, TOK_BLK), lambda b, e, m: (b, e, 0, 0))],
            out_specs=pl.BlockSpec((1, 1, cap, d_model), lambda b, e, m: (b, e, 0, 0)),
            scratch_shapes=[pltpu.VMEM((2, GATHER_ROWS, d_model), F32)]),
        out_shape=jax.ShapeDtypeStruct((bsz, n_e, cap, d_model), BF16),
        compiler_params=pltpu.CompilerParams(
            dimension_semantics=("parallel", "parallel"), vmem_limit_bytes=VMEM_LIMIT),
        name="gather",
    )(meta_flat, h2, pos4)


FFN_ROWS = 256


def _ffn_body(x_ref, wg_ref, wu_ref, wd_ref, y_ref):
    cap = x_ref.shape[2]
    for r in range(cap // FFN_ROWS):
        rows = slice(r * FFN_ROWS, (r + 1) * FFN_ROWS)
        xin = x_ref[0, 0, rows, :]
        a = _dot(xin, wg_ref[0])
        u = _dot(xin, wu_ref[0])
        mid = (a * jax.nn.sigmoid(a) * u).astype(BF16)
        y_ref[0, 0, rows, :] = _dot(mid, wd_ref[0]).astype(y_ref.dtype)


def _ffn(xin, w_gate, w_up, w_down):
    bsz, n_e, cap, d_model = xin.shape
    ff = w_gate.shape[2]
    tok = pl.BlockSpec((1, 1, cap, d_model), lambda e, b: (b, e, 0, 0))
    return pl.pallas_call(
        _ffn_body,
        grid=(n_e, bsz),
        in_specs=[tok,
                  pl.BlockSpec((1, d_model, ff), lambda e, b: (e, 0, 0)),
                  pl.BlockSpec((1, d_model, ff), lambda e, b: (e, 0, 0)),
                  pl.BlockSpec((1, ff, d_model), lambda e, b: (e, 0, 0))],
        out_specs=tok,
        out_shape=jax.ShapeDtypeStruct((bsz, n_e, cap, d_model), BF16),
        compiler_params=pltpu.CompilerParams(
            dimension_semantics=("parallel", "parallel"), vmem_limit_bytes=VMEM_LIMIT),
        name="ffn",
    )(xin, w_gate.astype(BF16), w_up.astype(BF16), w_down.astype(BF16))


SCATTER_ROWS = 256


def _scatter_body(meta_ref, y_ref, post_ref, gt_ref, x2_ref, gf_ref, o_ref, *, tc):
    b, t, e = pl.program_id(0), pl.program_id(1), pl.program_id(2)
    base = (b * N_EXPERTS + e) * LANES

    @pl.when(e == 0)
    def _():
        o_ref[0] = x2_ref[0]

    lane = lax.broadcasted_iota(jnp.int32, (1, LANES), 1)
    mine = lane == e
    slot = lax.broadcasted_iota(jnp.int32, (1, SCATTER_ROWS), 1)
    n_blk = tc // TOK_BLK
    last_tile = y_ref.shape[2] // SCATTER_ROWS - 1

    def block_range(kk):
        blk = t * n_blk + kk
        first = meta_ref[base + blk]
        last = meta_ref[base + blk + 1]
        return jnp.minimum(first // SCATTER_ROWS, last_tile), last

    def add_tile(kk, j):
        rows = slice(kk * TOK_BLK, (kk + 1) * TOK_BLK)
        pcol = jnp.sum(jnp.where(mine, post_ref[0, rows, :], 0.0), axis=1, keepdims=True)
        gcol = jnp.sum(jnp.where(mine, gt_ref[0, rows, :], 0.0), axis=1, keepdims=True)
        want = (slot + j * SCATTER_ROWS).astype(F32)
        onehot = jnp.where(pcol == want, 1.0, 0.0).astype(BF16)
        ytile = y_ref[0, 0, pl.ds(pl.multiple_of(j * SCATTER_ROWS, SCATTER_ROWS), SCATTER_ROWS), :]
        o_ref[0, rows, :] += gcol * _dot(onehot, ytile)

    for kk in range(n_blk):
        add_tile(kk, block_range(kk)[0])
    for kk in range(n_blk):
        tile0, last = block_range(kk)

        @pl.when(last > (tile0 + 1) * SCATTER_ROWS)
        def _():
            add_tile(kk, tile0 + 1)

    @pl.when(e == N_EXPERTS - 1)
    def _():
        v = o_ref[0]
        ms = jnp.mean(v * v, axis=-1, keepdims=True)
        o_ref[0] = v * lax.rsqrt(ms + NORM_EPS) * gf_ref[...]


def _scatter(meta_flat, y, post, gt, x2, normf_g, tc):
    bsz, n_e, cap, d_model = y.shape
    seq = x2.shape[1]
    tc = min(tc, seq)
    gf = normf_g[None, :]
    tokw = lambda w: pl.BlockSpec((1, tc, w), lambda b, t, e, m: (b, t, 0))
    return pl.pallas_call(
        functools.partial(_scatter_body, tc=tc),
        grid_spec=pltpu.PrefetchScalarGridSpec(
            num_scalar_prefetch=1,
            grid=(bsz, seq // tc, n_e),
            in_specs=[pl.BlockSpec((1, 1, cap, d_model), lambda b, t, e, m: (b, e, 0, 0)),
                      tokw(LANES), tokw(LANES), tokw(d_model),
                      pl.BlockSpec(gf.shape, lambda b, t, e, m: (0, 0))],
            out_specs=tokw(d_model)),
        out_shape=jax.ShapeDtypeStruct((bsz, seq, d_model), F32),
        compiler_params=pltpu.CompilerParams(
            dimension_semantics=("parallel", "parallel", "arbitrary"), vmem_limit_bytes=VMEM_LIMIT),
        name="scatter",
    )(meta_flat, y, post, gt, x2, gf)


def kernel(x, norm1_g, w_in, b_gates, q_norm_g, k_norm_g, w_o_a, w_o_b, w_out, norm2_g, w_router,
           w_gate, w_up, w_down, normf_g):
    bsz, seq, d_model = x.shape
    depth = norm1_g.shape[0]
    cap = EC_CAPACITY_FACTOR * seq // N_EXPERTS
    tm = min(512, seq)
    for l in range(depth):
        qa, ka, va, gate_a, gate_b, qkv_b = _in_proj(
            x, norm1_g[l], w_in[l], b_gates[l], q_norm_g[l], k_norm_g[l], tm)
        oa = _attn_a(qa, ka, va, tq=min(128, seq), tk=min(512, seq))
        obs, lses = [], []
        for gi, (_, dil) in enumerate(B_GROUPS):
            o, lse = _attn_b_group(*qkv_b[gi], gi, dil, tu=512)
            obs.append(o)
            lses.append(lse)
        x2, h2, aff = _merge(x, oa, obs, lses, gate_a, gate_b, w_o_a[l], w_o_b[l], w_out[l],
                             norm2_g[l], w_router[l], tm)
        pos, post, gt, meta = _route(aff, cap)
        meta_flat = meta.reshape(-1)
        xin = _gather(meta_flat, h2, pos, cap)
        y = _ffn(xin, w_gate[l], w_up[l], w_down[l])
        assert depth == 1
        x = _scatter(meta_flat, y, post, gt, x2, normf_g, tc=2048)
    return x
```

```python
import functools

import numpy as np
import jax
import jax.numpy as jnp
from jax import lax
from jax.experimental import pallas as pl
from jax.experimental.pallas import tpu as pltpu

HEAD_DIM = 64
A_Q_HEADS = 8
A_KV_HEADS = 2
A_GROUP = A_Q_HEADS // A_KV_HEADS
B_GROUPS = ((128, 1), (512, 4), (2048, 16))
B_HEADS_PER_GROUP = 4
B_HEADS = B_HEADS_PER_GROUP * len(B_GROUPS)
GRID_W = 64
ROPE_THETA = 10000.0
N_EXPERTS = 16
EC_CAPACITY_FACTOR = 2
NORM_EPS = 1e-6
NEG_INF = -1e30

A_Q_W = A_Q_HEADS * HEAD_DIM
A_KV_W = A_KV_HEADS * HEAD_DIM
B_W = B_HEADS * HEAD_DIM
B_OUT_W = B_HEADS_PER_GROUP * HEAD_DIM

LANES = 128
MXU_DIM = 256
VMEM_LIMIT = 56 * 1024 * 1024

F32 = jnp.float32
BF16 = jnp.bfloat16
LOG2_E = 1.4426950408889634


def _nt_dot(a, b):
    return lax.dot_general(a, b, (((1,), (1,)), ((), ())), preferred_element_type=F32)


def _dot(a, b):
    return jnp.dot(a, b, preferred_element_type=F32)


def _split_bf16(v):
    hi = v.astype(BF16)
    lo = (v - hi.astype(F32)).astype(BF16)
    return hi, lo


def _in_proj_body(x_ref, g1_ref, w_ref, bg_ref, qg_ref, kg_ref, cos_ref, sin_ref, hs_ref,
                  qa_ref, ka_ref, va_ref, ga_ref, gb_ref, *rest, d_model):
    b_refs, zb_scr = rest[:-1], rest[-1]
    tm = x_ref.shape[1]
    x = x_ref[0]
    ms = jnp.mean(x * x, axis=-1, keepdims=True)
    h = (x * lax.rsqrt(ms + NORM_EPS) * g1_ref[...]).astype(BF16)

    def proj(lo, width):
        return _dot(h, w_ref[:, lo:lo + width])

    cos = cos_ref[...]
    sin = sin_ref[...]
    lane = lax.broadcasted_iota(jnp.int32, (1, LANES), 1)
    low = (lane % 32) < 16

    def head_norm_rope(z, gain, nrep):
        width = LANES * nrep
        z2 = z * z
        hi, lo = _split_bf16(z2)
        hs = hs_ref[:width, :width]
        ss = _dot(hi, hs) + _dot(lo, hs)
        zn = z * lax.rsqrt(ss * (1.0 / HEAD_DIM) + NORM_EPS) * gain
        rep = lambda t: jnp.concatenate([t] * nrep, axis=1) if nrep > 1 else t
        partner = jnp.where(rep(low), pltpu.roll(zn, width - 16, 1), pltpu.roll(zn, 16, 1))
        return zn * rep(cos) + partner * rep(sin)

    scale = HEAD_DIM ** -0.5
    za = proj(0, A_Q_W + 2 * A_KV_W)
    qa = head_norm_rope(za[:, :A_Q_W], qg_ref[...], A_Q_W // LANES) * (scale * LOG2_E)
    ka = head_norm_rope(za[:, A_Q_W:A_Q_W + A_KV_W], kg_ref[...], A_KV_W // LANES)
    qa_ref[0] = qa.astype(BF16)
    ka_ref[0] = ka.astype(BF16)
    va = za[:, A_Q_W + A_KV_W:]
    va_ref[0] = jnp.concatenate([va, jnp.ones_like(va)], axis=1).astype(BF16)
    off = A_Q_W + 2 * A_KV_W
    slabs = B_OUT_W // LANES
    for t in range(3):
        z = proj(off + t * B_W, B_W) * (scale if t == 0 else 1.0)
        for c in range(B_W // LANES):
            zb_scr[c] = z[:, c * LANES:(c + 1) * LANES]
        for gi, (_, dil) in enumerate(B_GROUPS):
            out = b_refs[3 * gi + t]
            for r in range(dil):
                for c in range(slabs):
                    out[0, r, :, c * LANES:(c + 1) * LANES] = (
                        zb_scr[gi * slabs + c, pl.ds(r, tm // dil, stride=dil), :].astype(BF16))
    off += 3 * B_W
    ga_ref[0] = jax.nn.sigmoid(proj(off, d_model) + bg_ref[:, :d_model]).astype(BF16)
    gb_ref[0] = jax.nn.sigmoid(proj(off + d_model, d_model) + bg_ref[:, d_model:]).astype(BF16)


def _rope_tables(seq):
    rows = seq // GRID_W
    row_id = jnp.repeat(jnp.arange(rows, dtype=F32), GRID_W)
    col_id = jnp.tile(jnp.arange(GRID_W, dtype=F32), rows)
    half = HEAD_DIM // 2
    inv_freq = 1.0 / (ROPE_THETA ** (jnp.arange(0, half, 2, dtype=F32) / half))
    ang_r = row_id[:, None] * inv_freq[None, :]
    ang_c = col_id[:, None] * inv_freq[None, :]
    cos = jnp.concatenate([jnp.cos(ang_r)] * 2 + [jnp.cos(ang_c)] * 2, axis=1)
    sin = jnp.concatenate([-jnp.sin(ang_r), jnp.sin(ang_r), -jnp.sin(ang_c), jnp.sin(ang_c)], axis=1)
    return jnp.tile(cos, (1, 2)), jnp.tile(sin, (1, 2))


def _in_proj(x, norm1_g, w_in, b_gates, q_norm_g, k_norm_g, tm):
    bsz, seq, d_model = x.shape
    cos, sin = _rope_tables(seq)
    head_id = np.arange(A_Q_W) // HEAD_DIM
    hs = jnp.asarray(head_id[:, None] == head_id[None, :], BF16)
    qg = jnp.tile(q_norm_g, A_Q_W // HEAD_DIM)[None, :]
    kg = jnp.tile(k_norm_g, A_KV_W // HEAD_DIM)[None, :]
    row = lambda w: pl.BlockSpec((1, tm, w), lambda b, i: (b, i, 0))
    full = lambda a: pl.BlockSpec(a.shape, lambda b, i: (0,) * a.ndim)
    tab = pl.BlockSpec((tm, LANES), lambda b, i: (i, 0))
    g1 = norm1_g[None, :]
    bg = b_gates[None, :]
    w = w_in.astype(BF16)
    shp = lambda wd: jax.ShapeDtypeStruct((bsz, seq, wd), BF16)
    b_specs, b_shapes = [], []
    for _, dil in B_GROUPS:
        assert tm % (16 * dil) == 0
        b_specs += [pl.BlockSpec((1, dil, tm // dil, B_OUT_W), lambda b, i: (b, 0, i, 0))] * 3
        b_shapes += [jax.ShapeDtypeStruct((bsz, dil, seq // dil, B_OUT_W), BF16)] * 3
    outs = pl.pallas_call(
        functools.partial(_in_proj_body, d_model=d_model),
        grid=(bsz, seq // tm),
        in_specs=[row(d_model), full(g1), full(w), full(bg), full(qg), full(kg), tab, tab, full(hs)],
        out_specs=[row(A_Q_W), row(A_KV_W), row(2 * A_KV_W), row(d_model), row(d_model)] + b_specs,
        out_shape=[shp(A_Q_W), shp(A_KV_W), shp(2 * A_KV_W), shp(d_model), shp(d_model)] + b_shapes,
        scratch_shapes=[pltpu.VMEM((B_W // LANES, tm, LANES), F32)],
        compiler_params=pltpu.CompilerParams(
            dimension_semantics=("parallel", "parallel"), vmem_limit_bytes=VMEM_LIMIT),
        name="in_proj",
    )(x, g1, w, bg, qg, kg, cos, sin, hs)
    qa, ka, va, gate_a, gate_b = outs[:5]
    qkv_b = [outs[5 + 3 * gi:8 + 3 * gi] for gi in range(len(B_GROUPS))]
    return qa, ka, va, gate_a, gate_b, qkv_b


def _attn_a_body(q_ref, k_ref, v_ref, o_ref, qp_scr, s_scr, p_scr, m_scr, a_scr, acc_scr, *, tk):
    tq = q_ref.shape[1]
    seq = k_ref.shape[1]
    nk = seq // tk
    rows_all = A_Q_HEADS * tq
    q = q_ref[0].astype(F32)
    zeros = jnp.zeros((tq, HEAD_DIM), F32)
    for g in range(A_Q_HEADS):
        qg = q[:, g * HEAD_DIM:(g + 1) * HEAD_DIM]
        qp = jnp.concatenate([qg, zeros] if g // A_GROUP == 0 else [zeros, qg], axis=1)
        qp_scr[g * tq:(g + 1) * tq, :] = qp.astype(BF16)
    m_scr[...] = jnp.full_like(m_scr, NEG_INF)
    acc_scr[...] = jnp.zeros_like(acc_scr)

    def scores(kt, slot, part=None):
        start = pl.multiple_of(kt * tk, tk)
        if part is None:
            s_scr[slot] = _nt_dot(qp_scr[...], k_ref[0, pl.ds(start, tk), :])
        else:
            rows = slice(part * ATTN_A_SCORE_ROWS, (part + 1) * ATTN_A_SCORE_ROWS)
            s_scr[slot, rows, :] = _nt_dot(qp_scr[rows, :], k_ref[0, pl.ds(start, tk), :])

    def softmax_pv(kt, slot, next_kt=None, next_slot=None):
        per = ATTN_A_SCORE_ROWS // ATTN_A_CHUNK
        for c in range(rows_all // ATTN_A_CHUNK):
            if next_kt is not None and c % per == 0:
                scores(next_kt, next_slot, c // per)
            rows = slice(c * ATTN_A_CHUNK, (c + 1) * ATTN_A_CHUNK)
            s = s_scr[slot, rows, :]
            m_prev = m_scr[rows, :]
            m_new = jnp.maximum(m_prev, jnp.max(s, axis=1, keepdims=True))
            m_scr[rows, :] = m_new
            a_scr[slot, rows, :] = jnp.exp2(m_prev - m_new)
            d = s - jnp.concatenate([m_new] * (tk // LANES), axis=1)
            p_scr[slot, rows, :] = jnp.exp2(d.astype(BF16))
        start = pl.multiple_of(kt * tk, tk)
        pv = _dot(p_scr[slot], v_ref[0, pl.ds(start, tk), :])
        alpha = a_scr[slot]
        acc_scr[...] = jnp.concatenate([alpha, alpha], axis=1) * acc_scr[...] + pv

    scores(0, 0)

    def trip(i, carry):
        for u in range(ATTN_A_UNROLL):
            kt = ATTN_A_UNROLL * i + u
            softmax_pv(kt, u % 2, kt + 1, (u + 1) % 2)
        return carry

    lax.fori_loop(0, nk // ATTN_A_UNROLL - 1, trip, 0)
    for kt in range(nk - ATTN_A_UNROLL, nk):
        if kt + 1 < nk:
            softmax_pv(kt, kt % 2, kt + 1, (kt + 1) % 2)
        else:
            softmax_pv(kt, kt % 2)

    outs = []
    for g in range(A_Q_HEADS):
        kv = g // A_GROUP
        rows = slice(g * tq, (g + 1) * tq)
        outs.append(acc_scr[rows, kv * HEAD_DIM:(kv + 1) * HEAD_DIM]
                    / acc_scr[rows, A_KV_W:A_KV_W + HEAD_DIM])
    o_ref[0] = jnp.concatenate(outs, axis=1).astype(o_ref.dtype)


ATTN_A_CHUNK = 128
ATTN_A_SCORE_ROWS = 256
ATTN_A_UNROLL = 4


def _attn_a(qa, ka, va, tq, tk):
    bsz, seq, _ = qa.shape
    assert (seq // tk) % ATTN_A_UNROLL == 0
    rows_all = A_Q_HEADS * tq
    return pl.pallas_call(
        functools.partial(_attn_a_body, tk=tk),
        grid=(bsz, seq // tq),
        in_specs=[pl.BlockSpec((1, tq, A_Q_W), lambda b, i: (b, i, 0)),
                  pl.BlockSpec((1, seq, A_KV_W), lambda b, i: (b, 0, 0)),
                  pl.BlockSpec((1, seq, 2 * A_KV_W), lambda b, i: (b, 0, 0))],
        out_specs=pl.BlockSpec((1, tq, A_Q_W), lambda b, i: (b, i, 0)),
        out_shape=jax.ShapeDtypeStruct((bsz, seq, A_Q_W), BF16),
        scratch_shapes=[pltpu.VMEM((rows_all, A_KV_W), BF16),
                        pltpu.VMEM((2, rows_all, tk), F32),
                        pltpu.VMEM((2, rows_all, tk), BF16),
                        pltpu.VMEM((rows_all, LANES), F32),
                        pltpu.VMEM((2, rows_all, LANES), F32),
                        pltpu.VMEM((rows_all, 2 * A_KV_W), F32)],
        compiler_params=pltpu.CompilerParams(
            dimension_semantics=("parallel", "parallel"), vmem_limit_bytes=VMEM_LIMIT),
        name="attn_a",
    )(qa, ka, va)


HALO = 128
SUB = 128
N_SIDE = 64
WIN = SUB + 2 * N_SIDE


def _attn_b_body(q_ref, kp_ref, kc_ref, kn_ref, vp_ref, vc_ref, vn_ref, bias_ref, o_ref, lse_ref,
                 kbuf, vbuf, *, seq_d):
    tu = q_ref.shape[2]
    i = pl.program_id(2)
    kbuf[0:HALO] = kp_ref[0, 0]
    kbuf[HALO:HALO + tu] = kc_ref[0, 0]
    kbuf[HALO + tu:] = kn_ref[0, 0]
    vbuf[0:HALO] = vp_ref[0, 0]
    vbuf[HALO:HALO + tu] = vc_ref[0, 0]
    vbuf[HALO + tu:] = vn_ref[0, 0]

    nh = B_HEADS_PER_GROUP
    nsub = tu // SUB
    head_of_lane = lax.broadcasted_iota(jnp.int32, (1, B_OUT_W), 1) // HEAD_DIM
    col = lax.broadcasted_iota(jnp.int32, (1, WIN), 1)

    def scores(j):
        qs = q_ref[0, 0, j * SUB:(j + 1) * SUB, :]
        qst = jnp.concatenate([jnp.where(head_of_lane == hh, qs, jnp.zeros_like(qs)) for hh in range(nh)],
                              axis=0)
        w0 = HALO - N_SIDE + j * SUB
        return _nt_dot(qst, kbuf[w0:w0 + WIN, :])

    s_next = scores(0)
    for j in range(nsub):
        s = s_next + bias_ref[...]
        if j + 1 < nsub:
            s_next = scores(j + 1)
        if j == 0 or j == nsub - 1:
            key_u = i * tu + j * SUB - N_SIDE + col
            s = jnp.where((key_u >= 0) & (key_u < seq_d), s, NEG_INF)
        m = jnp.max(s, axis=1, keepdims=True)
        p = jnp.exp(s - m)
        l = jnp.sum(p, axis=1, keepdims=True)
        w0 = HALO - N_SIDE + j * SUB
        pv = _dot(p.astype(BF16), vbuf[w0:w0 + WIN, :])
        on = pv / l
        lse = m + jnp.log(l)
        o_acc = jnp.zeros((SUB, B_OUT_W), F32)
        lse_acc = jnp.zeros((SUB, B_OUT_W), F32)
        for hh in range(nh):
            rows = slice(hh * SUB, (hh + 1) * SUB)
            mine = head_of_lane == hh
            o_acc = jnp.where(mine, on[rows], o_acc)
            lse_acc = jnp.where(mine, lse[rows], lse_acc)
        o_ref[0, 0, j * SUB:(j + 1) * SUB, :] = o_acc
        lse_ref[0, 0, j * SUB:(j + 1) * SUB, :] = lse_acc


def _band_bias(group, dil):
    row = np.arange(SUB)[:, None]
    col = np.arange(WIN)[None, :]
    rel = col - N_SIDE - row
    tiles = []
    for hh in range(B_HEADS_PER_GROUP):
        slope = 2.0 ** (-8.0 * (group * B_HEADS_PER_GROUP + hh + 1) / B_HEADS)
        tiles.append(np.where(np.abs(rel) <= N_SIDE, -slope * np.abs(rel) * dil, NEG_INF))
    return jnp.asarray(np.concatenate(tiles, axis=0), F32)


def _attn_b_group(q, k, v, group, dil, tu):
    bsz, _, seq_d, _ = q.shape
    tu = min(tu, seq_d)
    per = tu // HALO
    last = seq_d // HALO - 1
    bias = _band_bias(group, dil)
    cur = pl.BlockSpec((1, 1, tu, B_OUT_W), lambda b, r, i: (b, r, i, 0))
    prev = pl.BlockSpec((1, 1, HALO, B_OUT_W), lambda b, r, i: (b, r, jnp.maximum(i * per - 1, 0), 0))
    nxt = pl.BlockSpec((1, 1, HALO, B_OUT_W), lambda b, r, i: (b, r, jnp.minimum((i + 1) * per, last), 0))
    return pl.pallas_call(
        functools.partial(_attn_b_body, seq_d=seq_d),
        grid=(bsz, dil, seq_d // tu),
        in_specs=[cur, prev, cur, nxt, prev, cur, nxt,
                  pl.BlockSpec(bias.shape, lambda b, r, i: (0, 0))],
        out_specs=[cur, cur],
        out_shape=[jax.ShapeDtypeStruct((bsz, dil, seq_d, B_OUT_W), F32)] * 2,
        scratch_shapes=[pltpu.VMEM((tu + 2 * HALO, B_OUT_W), BF16)] * 2,
        compiler_params=pltpu.CompilerParams(
            dimension_semantics=("parallel", "parallel", "parallel"), vmem_limit_bytes=VMEM_LIMIT),
        name=f"attn_b{group}",
    )(q, k, k, k, v, v, v, bias)


def _merge_body(x_ref, oa_ref, o0_ref, o1_ref, o2_ref, l0_ref, l1_ref, l2_ref, ga_ref, gb_ref,
                woa_ref, wob_ref, wout_ref, g2_ref, wrh_ref, wrl_ref,
                x2_ref, h2_ref, aff_ref, o_scr, l_scr):
    tm = x_ref.shape[1]
    slabs = B_OUT_W // LANES
    for gi, (o_ref, lse_ref) in enumerate(((o0_ref, l0_ref), (o1_ref, l1_ref), (o2_ref, l2_ref))):
        dil = B_GROUPS[gi][1]
        for r in range(dil):
            for c in range(slabs):
                lanes = slice(c * LANES, (c + 1) * LANES)
                o_scr[gi * slabs + c, pl.ds(r, tm // dil, stride=dil), :] = o_ref[0, r, :, lanes]
                l_scr[gi * slabs + c, pl.ds(r, tm // dil, stride=dil), :] = lse_ref[0, r, :, lanes]
    whole = lambda scr, gi: jnp.concatenate([scr[gi * slabs + c] for c in range(slabs)], axis=1)
    l0, l1, l2 = whole(l_scr, 0), whole(l_scr, 1), whole(l_scr, 2)
    mx = jnp.maximum(jnp.maximum(l0, l1), l2)
    e0, e1, e2 = jnp.exp(l0 - mx), jnp.exp(l1 - mx), jnp.exp(l2 - mx)
    ob = (e0 * whole(o_scr, 0) + e1 * whole(o_scr, 1) + e2 * whole(o_scr, 2)) / (e0 + e1 + e2)
    ya = _dot(oa_ref[0], woa_ref[...])
    yb = _dot(ob.astype(BF16), wob_ref[...])
    mrg = ga_ref[0].astype(F32) * ya + gb_ref[0].astype(F32) * yb
    x2 = x_ref[0] + _dot(mrg.astype(BF16), wout_ref[...])
    x2_ref[0] = x2
    ms = jnp.mean(x2 * x2, axis=-1, keepdims=True)
    h2 = x2 * lax.rsqrt(ms + NORM_EPS) * g2_ref[...]
    h2_ref[0] = h2.astype(BF16)
    hh, hl = _split_bf16(h2)
    wh, wl = wrh_ref[...], wrl_ref[...]
    logits = _nt_dot(wh, hh) + _nt_dot(wh, hl) + _nt_dot(wl, hh)
    mxl = jnp.max(logits, axis=0, keepdims=True)
    ex = jnp.exp(logits - mxl)
    aff_ref[0] = ex / jnp.sum(ex, axis=0, keepdims=True)


def _merge(x, oa, obs, lses, gate_a, gate_b, w_o_a, w_o_b, w_out, norm2_g, w_router, tm):
    bsz, seq, d_model = x.shape
    row = lambda w: pl.BlockSpec((1, tm, w), lambda b, i: (b, i, 0))
    full = lambda a: pl.BlockSpec(a.shape, lambda b, i: (0,) * a.ndim)
    res = [pl.BlockSpec((1, dil, tm // dil, B_OUT_W), lambda b, i: (b, 0, i, 0)) for _, dil in B_GROUPS]
    woa, wob, wout = w_o_a.astype(BF16), w_o_b.astype(BF16), w_out.astype(BF16)
    g2 = norm2_g[None, :]
    wrh, wrl = _split_bf16(w_router.T)
    return pl.pallas_call(
        _merge_body,
        grid=(bsz, seq // tm),
        in_specs=[row(d_model), row(A_Q_W)] + res + res + [row(d_model)] * 2
                 + [full(woa), full(wob), full(wout), full(g2), full(wrh), full(wrl)],
        out_specs=[row(d_model), row(d_model),
                   pl.BlockSpec((1, N_EXPERTS, tm), lambda b, i: (b, 0, i))],
        out_shape=[jax.ShapeDtypeStruct((bsz, seq, d_model), F32),
                   jax.ShapeDtypeStruct((bsz, seq, d_model), BF16),
                   jax.ShapeDtypeStruct((bsz, N_EXPERTS, seq), F32)],
        scratch_shapes=[pltpu.VMEM((len(B_GROUPS) * B_OUT_W // LANES, tm, LANES), F32)] * 2,
        compiler_params=pltpu.CompilerParams(
            dimension_semantics=("parallel", "parallel"), vmem_limit_bytes=VMEM_LIMIT),
        name="merge",
    )(x, oa, *obs, *lses, gate_a, gate_b, woa, wob, wout, g2, wrh, wrl)


TOK_BLK = 256
GATHER_ROWS = 128
META_LO = 64
META_HI = 96


def _route_body(aff_ref, pos_ref, post_ref, gt_ref, meta_ref, *, cap):
    aff = aff_ref[0]
    n_e, seq = aff.shape
    bits = pltpu.bitcast(aff, jnp.int32)

    def refine(it, thr):
        cand = thr | jnp.left_shift(jnp.int32(1), 30 - it)
        cnt = jnp.sum(jnp.where(bits >= cand, 1.0, 0.0), axis=1, keepdims=True)
        return jnp.where(cnt >= cap, cand, thr)

    thr = lax.fori_loop(0, 31, refine, jnp.zeros((n_e, 1), jnp.int32))
    above = bits > thr
    tied = bits == thr
    need = cap - jnp.sum(jnp.where(above, 1.0, 0.0), axis=1, keepdims=True)

    ri = lax.broadcasted_iota(jnp.int32, (TOK_BLK, TOK_BLK), 0)
    ci = lax.broadcasted_iota(jnp.int32, (TOK_BLK, TOK_BLK), 1)
    before = jnp.where(ri < ci, 1.0, 0.0).astype(BF16)
    lane = lax.broadcasted_iota(jnp.int32, (1, LANES), 1)
    pad_rows = LANES - n_e

    run_tied = jnp.zeros((n_e, 1), F32)
    run_sel = jnp.zeros((n_e, 1), F32)
    starts = jnp.zeros((n_e, LANES), F32)
    ends = jnp.zeros((n_e, LANES), F32)
    n_blk = seq // TOK_BLK
    for c in range(n_blk):
        sl = slice(c * TOK_BLK, (c + 1) * TOK_BLK)
        tied_c = jnp.where(tied[:, sl], 1.0, 0.0)
        rank = run_tied + _dot(tied_c.astype(BF16), before)
        sel = above[:, sl] | (tied[:, sl] & (rank < need))
        sel_c = jnp.where(sel, 1.0, 0.0)
        pos = run_sel + _dot(sel_c.astype(BF16), before)
        pos_m = jnp.where(sel, pos, -1.0)
        g_m = jnp.where(sel, aff[:, sl], 0.0)
        pos_ref[0, :, sl] = pos_m
        fill = jnp.full((pad_rows, TOK_BLK), -1.0, F32)
        post_ref[0, sl, :] = jnp.concatenate([pos_m, fill], axis=0).T
        gt_ref[0, sl, :] = jnp.concatenate([g_m, jnp.zeros((pad_rows, TOK_BLK), F32)], axis=0).T
        starts = jnp.where(lane == c, run_sel, starts)
        run_tied = run_tied + jnp.sum(tied_c, axis=1, keepdims=True)
        run_sel = run_sel + jnp.sum(sel_c, axis=1, keepdims=True)
        ends = jnp.where(lane == c, run_sel, ends)
    meta = jnp.where(lane == n_blk, run_sel, starts)
    real = lane < n_blk
    for j in range(cap // GATHER_ROWS):
        c0 = float(j * GATHER_ROWS)
        first = jnp.sum(jnp.where(real & (ends <= c0), 1.0, 0.0), axis=1, keepdims=True)
        stop = n_blk - jnp.sum(jnp.where(real & (starts >= c0 + GATHER_ROWS), 1.0, 0.0), axis=1, keepdims=True)
        meta = jnp.where(lane == META_LO + j, first, meta)
        meta = jnp.where(lane == META_HI + j, stop, meta)
    meta_ref[0] = meta.astype(jnp.int32)


def _route(aff, cap):
    bsz, n_e, seq = aff.shape
    return pl.pallas_call(
        functools.partial(_route_body, cap=cap),
        grid=(bsz,),
        in_specs=[pl.BlockSpec((1, n_e, seq), lambda b: (b, 0, 0))],
        out_specs=[pl.BlockSpec((1, n_e, seq), lambda b: (b, 0, 0)),
                   pl.BlockSpec((1, seq, LANES), lambda b: (b, 0, 0)),
                   pl.BlockSpec((1, seq, LANES), lambda b: (b, 0, 0)),
                   pl.BlockSpec((1, n_e, LANES), lambda b: (b, 0, 0))],
        out_shape=[jax.ShapeDtypeStruct((bsz, n_e, seq), F32),
                   jax.ShapeDtypeStruct((bsz, seq, LANES), F32),
                   jax.ShapeDtypeStruct((bsz, seq, LANES), F32),
                   jax.ShapeDtypeStruct((bsz, n_e, LANES), jnp.int32)],
        compiler_params=pltpu.CompilerParams(
            dimension_semantics=("parallel",), vmem_limit_bytes=VMEM_LIMIT),
        name="route",
    )(aff)


GATHER_SPAN = 6


def _gather_body(meta_ref, h_ref, pos_ref, xin_ref, acc_ref, *, cap):
    b, e = pl.program_id(0), pl.program_id(1)
    base = (b * N_EXPERTS + e) * LANES
    n_blk = pos_ref.shape[2]
    half = GATHER_SPAN // 2
    slot = lax.broadcasted_iota(jnp.int32, (GATHER_ROWS, 1), 0)
    for j in range(cap // GATHER_ROWS):
        want = (slot + j * GATHER_ROWS).astype(F32)
        acc_ref[...] = jnp.zeros_like(acc_ref)
        first = meta_ref[base + META_LO + j]
        stop = meta_ref[base + META_HI + j]

        def add_blocks(i, carry, want=want, first=first):
            begin = first + GATHER_SPAN * i
            begin_c = jnp.minimum(begin, n_blk - GATHER_SPAN)
            for c in range(2):
                blk0 = begin_c + c * half
                pieces = []
                for t in range(half):
                    pos = pos_ref[0, 0, pl.ds(blk0 + t, 1), :]
                    pos = jnp.where(blk0 + t >= begin, pos, -1.0)
                    pieces.append(jnp.where(pos == want, 1.0, 0.0))
                onehot = jnp.concatenate(pieces, axis=1).astype(BF16)
                hblk = h_ref[0, pl.ds(pl.multiple_of(blk0 * TOK_BLK, TOK_BLK), half * TOK_BLK), :]
                acc_ref[c] += _dot(onehot, hblk)
            return carry

        lax.fori_loop(0, (stop - first + GATHER_SPAN - 1) // GATHER_SPAN, add_blocks, 0)
        xin_ref[0, 0, j * GATHER_ROWS:(j + 1) * GATHER_ROWS, :] = (acc_ref[0] + acc_ref[1]).astype(xin_ref.dtype)


def _gather(meta_flat, h2, pos, cap):
    bsz, seq, d_model = h2.shape
    n_e = pos.shape[1]
    assert seq // TOK_BLK >= GATHER_SPAN
    pos4 = pos.reshape(bsz, n_e, seq // TOK_BLK, TOK_BLK)
    return pl.pallas_call(
        functools.partial(_gather_body, cap=cap),
        grid_spec=pltpu.PrefetchScalarGridSpec(
            num_scalar_prefetch=1,
            grid=(bsz, n_e),
            in_specs=[pl.BlockSpec((1, seq, d_model), lambda b, e, m: (b, 0, 0)),
                      pl.BlockSpec((1, 1, seq // TOK_BLK, TOK_BLK), lambda b, e, m: (b, e, 0, 0))],
            out_specs=pl.BlockSpec((1, 1, cap, d_model), lambda b, e, m: (b, e, 0, 0)),
            scratch_shapes=[pltpu.VMEM((2, GATHER_ROWS, d_model), F32)]),
        out_shape=jax.ShapeDtypeStruct((bsz, n_e, cap, d_model), BF16),
        compiler_params=pltpu.CompilerParams(
            dimension_semantics=("parallel", "parallel"), vmem_limit_bytes=VMEM_LIMIT),
        name="gather",
    )(meta_flat, h2, pos4)


FFN_ROWS = 256


def _ffn_body(x_ref, wg_ref, wu_ref, wd_ref, y_ref):
    cap = x_ref.shape[2]
    for r in range(cap // FFN_ROWS):
        rows = slice(r * FFN_ROWS, (r + 1) * FFN_ROWS)
        xin = x_ref[0, 0, rows, :]
        a = _dot(xin, wg_ref[0])
        u = _dot(xin, wu_ref[0])
        mid = (a * jax.nn.sigmoid(a) * u).astype(BF16)
        y_ref[0, 0, rows, :] = _dot(mid, wd_ref[0]).astype(y_ref.dtype)


def _ffn(xin, w_gate, w_up, w_down):
    bsz, n_e, cap, d_model = xin.shape
    ff = w_gate.shape[2]
    tok = pl.BlockSpec((1, 1, cap, d_model), lambda e, b: (b, e, 0, 0))
    return pl.pallas_call(
        _ffn_body,
        grid=(n_e, bsz),
        in_specs=[tok,
                  pl.BlockSpec((1, d_model, ff), lambda e, b: (e, 0, 0)),
                  pl.BlockSpec((1, d_model, ff), lambda e, b: (e, 0, 0)),
                  pl.BlockSpec((1, ff, d_model), lambda e, b: (e, 0, 0))],
        out_specs=tok,
        out_shape=jax.ShapeDtypeStruct((bsz, n_e, cap, d_model), BF16),
        compiler_params=pltpu.CompilerParams(
            dimension_semantics=("parallel", "parallel"), vmem_limit_bytes=VMEM_LIMIT),
        name="ffn",
    )(xin, w_gate.astype(BF16), w_up.astype(BF16), w_down.astype(BF16))


SCATTER_WIN = 128
SCATTER_ALIGN = 64
SCATTER_EXPERTS = 4


def _scatter_body(meta_ref, y_ref, post_ref, gt_ref, x2_ref, gf_ref, o_ref, *, tc):
    b, t, eg = pl.program_id(0), pl.program_id(1), pl.program_id(2)
    cap = y_ref.shape[2]
    n_blk = tc // TOK_BLK

    @pl.when(eg == 0)
    def _():
        o_ref[0] = x2_ref[0]

    lane = lax.broadcasted_iota(jnp.int32, (1, LANES), 1)
    slot = lax.broadcasted_iota(jnp.int32, (1, SCATTER_WIN), 1)

    def expert_range(kk, el):
        base = (b * N_EXPERTS + eg * SCATTER_EXPERTS + el) * LANES + t * n_blk + kk
        first = meta_ref[base]
        last = meta_ref[base + 1]
        start = jnp.minimum((first // SCATTER_ALIGN) * SCATTER_ALIGN, cap - SCATTER_WIN)
        return start, last

    def columns(kk, el):
        rows = slice(kk * TOK_BLK, (kk + 1) * TOK_BLK)
        mine = lane == eg * SCATTER_EXPERTS + el
        pcol = jnp.sum(jnp.where(mine, post_ref[0, rows, :], 0.0), axis=1, keepdims=True)
        gcol = jnp.sum(jnp.where(mine, gt_ref[0, rows, :], 0.0), axis=1, keepdims=True)
        return pcol, gcol

    def window(el, start):
        return y_ref[0, el, pl.ds(pl.multiple_of(start, SCATTER_ALIGN), SCATTER_WIN), :]

    for kk in range(n_blk):
        rows = slice(kk * TOK_BLK, (kk + 1) * TOK_BLK)
        for pair in range(SCATTER_EXPERTS // 2):
            sel, ywin = [], []
            for el in (2 * pair, 2 * pair + 1):
                start, _ = expert_range(kk, el)
                pcol, gcol = columns(kk, el)
                sel.append(jnp.where(pcol == (slot + start).astype(F32), gcol, 0.0))
                ywin.append(window(el, start))
            o_ref[0, rows, :] += _dot(jnp.concatenate(sel, axis=1).astype(BF16), jnp.concatenate(ywin, axis=0))

    for kk in range(n_blk):
        rows = slice(kk * TOK_BLK, (kk + 1) * TOK_BLK)
        for el in range(SCATTER_EXPERTS):
            start, last = expert_range(kk, el)

            @pl.when(last > start + SCATTER_WIN)
            def _(kk=kk, el=el, rows=rows, start=start, last=last):
                pcol, gcol = columns(kk, el)

                def more(j, carry):
                    begin = start + (j + 1) * SCATTER_WIN
                    begin_c = jnp.minimum(begin, cap - SCATTER_WIN)
                    want = slot + begin_c
                    sel = jnp.where((pcol == want.astype(F32)) & (want >= begin), gcol, 0.0)
                    o_ref[0, rows, :] += _dot(sel.astype(BF16), window(el, begin_c))
                    return carry

                lax.fori_loop(0, (last - start - 1) // SCATTER_WIN, more, 0)

    @pl.when(eg == pl.num_programs(2) - 1)
    def _():
        v = o_ref[0]
        ms = jnp.mean(v * v, axis=-1, keepdims=True)
        o_ref[0] = v * lax.rsqrt(ms + NORM_EPS) * gf_ref[...]


def _scatter(meta_flat, y, post, gt, x2, normf_g, tc):
    bsz, n_e, cap, d_model = y.shape
    seq = x2.shape[1]
    tc = min(tc, seq)
    gf = normf_g[None, :]
    assert n_e % SCATTER_EXPERTS == 0 and cap % SCATTER_ALIGN == 0 and cap >= SCATTER_WIN
    tokw = lambda w, **kw: pl.BlockSpec((1, tc, w), lambda b, t, e, m: (b, t, 0), **kw)
    return pl.pallas_call(
        functools.partial(_scatter_body, tc=tc),
        grid_spec=pltpu.PrefetchScalarGridSpec(
            num_scalar_prefetch=1,
            grid=(bsz, seq // tc, n_e // SCATTER_EXPERTS),
            in_specs=[pl.BlockSpec((1, SCATTER_EXPERTS, cap, d_model), lambda b, t, e, m: (b, e, 0, 0)),
                      tokw(LANES), tokw(LANES),
                      tokw(d_model, pipeline_mode=pl.Buffered(1)),
                      pl.BlockSpec(gf.shape, lambda b, t, e, m: (0, 0))],
            out_specs=tokw(d_model)),
        out_shape=jax.ShapeDtypeStruct((bsz, seq, d_model), F32),
        compiler_params=pltpu.CompilerParams(
            dimension_semantics=("parallel", "parallel", "arbitrary"), vmem_limit_bytes=VMEM_LIMIT),
        name="scatter",
    )(meta_flat, y, post, gt, x2, gf)


def kernel(x, norm1_g, w_in, b_gates, q_norm_g, k_norm_g, w_o_a, w_o_b, w_out, norm2_g, w_router,
           w_gate, w_up, w_down, normf_g):
    bsz, seq, d_model = x.shape
    depth = norm1_g.shape[0]
    cap = EC_CAPACITY_FACTOR * seq // N_EXPERTS
    tm = min(512, seq)
    for l in range(depth):
        qa, ka, va, gate_a, gate_b, qkv_b = _in_proj(
            x, norm1_g[l], w_in[l], b_gates[l], q_norm_g[l], k_norm_g[l], tm)
        oa = _attn_a(qa, ka, va, tq=min(128, seq), tk=min(512, seq))
        obs, lses = [], []
        for gi, (_, dil) in enumerate(B_GROUPS):
            o, lse = _attn_b_group(*qkv_b[gi], gi, dil, tu=512)
            obs.append(o)
            lses.append(lse)
        x2, h2, aff = _merge(x, oa, obs, lses, gate_a, gate_b, w_o_a[l], w_o_b[l], w_out[l],
                             norm2_g[l], w_router[l], tm)
        pos, post, gt, meta = _route(aff, cap)
        meta_flat = meta.reshape(-1)
        xin = _gather(meta_flat, h2, pos, cap)
        y = _ffn(xin, w_gate[l], w_up[l], w_down[l])
        assert depth == 1
        x = _scatter(meta_flat, y, post, gt, x2, normf_g, tc=2048)
    return x
```

```python
import functools

import numpy as np
import jax
import jax.numpy as jnp
from jax import lax
from jax.experimental import pallas as pl
from jax.experimental.pallas import tpu as pltpu

HEAD_DIM = 64
A_Q_HEADS = 8
A_KV_HEADS = 2
A_GROUP = A_Q_HEADS // A_KV_HEADS
B_GROUPS = ((128, 1), (512, 4), (2048, 16))
B_HEADS_PER_GROUP = 4
B_HEADS = B_HEADS_PER_GROUP * len(B_GROUPS)
GRID_W = 64
ROPE_THETA = 10000.0
N_EXPERTS = 16
EC_CAPACITY_FACTOR = 2
NORM_EPS = 1e-6
NEG_INF = -1e30

A_Q_W = A_Q_HEADS * HEAD_DIM
A_KV_W = A_KV_HEADS * HEAD_DIM
B_W = B_HEADS * HEAD_DIM
B_OUT_W = B_HEADS_PER_GROUP * HEAD_DIM

LANES = 128
MXU_DIM = 256
VMEM_LIMIT = 56 * 1024 * 1024

F32 = jnp.float32
BF16 = jnp.bfloat16
LOG2_E = 1.4426950408889634


def _nt_dot(a, b):
    return lax.dot_general(a, b, (((1,), (1,)), ((), ())), preferred_element_type=F32)


def _dot(a, b):
    return jnp.dot(a, b, preferred_element_type=F32)


def _split_bf16(v):
    hi = v.astype(BF16)
    lo = (v - hi.astype(F32)).astype(BF16)
    return hi, lo


def _in_proj_body(x_ref, g1_ref, w_ref, bg_ref, qg_ref, kg_ref, cos_ref, sin_ref, hs_ref,
                  qa_ref, ka_ref, va_ref, ga_ref, gb_ref, *rest, d_model):
    b_refs, zb_scr = rest[:-1], rest[-1]
    tm = x_ref.shape[1]
    x = x_ref[0]
    ms = jnp.mean(x * x, axis=-1, keepdims=True)
    h = (x * lax.rsqrt(ms + NORM_EPS) * g1_ref[...]).astype(BF16)

    def proj(lo, width):
        return _dot(h, w_ref[:, lo:lo + width])

    cos = cos_ref[...]
    sin = sin_ref[...]
    lane = lax.broadcasted_iota(jnp.int32, (1, LANES), 1)
    low = (lane % 32) < 16

    def head_norm_rope(z, gain, nrep):
        width = LANES * nrep
        z2 = z * z
        hi, lo = _split_bf16(z2)
        hs = hs_ref[:width, :width]
        ss = _dot(hi, hs) + _dot(lo, hs)
        zn = z * lax.rsqrt(ss * (1.0 / HEAD_DIM) + NORM_EPS) * gain
        rep = lambda t: jnp.concatenate([t] * nrep, axis=1) if nrep > 1 else t
        partner = jnp.where(rep(low), pltpu.roll(zn, width - 16, 1), pltpu.roll(zn, 16, 1))
        return zn * rep(cos) + partner * rep(sin)

    scale = HEAD_DIM ** -0.5
    za = proj(0, A_Q_W + 2 * A_KV_W)
    qa = head_norm_rope(za[:, :A_Q_W], qg_ref[...], A_Q_W // LANES) * (scale * LOG2_E)
    ka = head_norm_rope(za[:, A_Q_W:A_Q_W + A_KV_W], kg_ref[...], A_KV_W // LANES)
    qa_ref[0] = qa.astype(BF16)
    ka_ref[0] = ka.astype(BF16)
    va = za[:, A_Q_W + A_KV_W:]
    va_ref[0] = jnp.concatenate([va, jnp.ones_like(va)], axis=1).astype(BF16)
    off = A_Q_W + 2 * A_KV_W
    slabs = B_OUT_W // LANES
    for t in range(3):
        z = proj(off + t * B_W, B_W) * (scale if t == 0 else 1.0)
        for c in range(B_W // LANES):
            zb_scr[c] = z[:, c * LANES:(c + 1) * LANES]
        for gi, (_, dil) in enumerate(B_GROUPS):
            out = b_refs[3 * gi + t]
            for r in range(dil):
                for c in range(slabs):
                    out[0, r, :, c * LANES:(c + 1) * LANES] = (
                        zb_scr[gi * slabs + c, pl.ds(r, tm // dil, stride=dil), :].astype(BF16))
    off += 3 * B_W
    ga_ref[0] = jax.nn.sigmoid(proj(off, d_model) + bg_ref[:, :d_model]).astype(BF16)
    gb_ref[0] = jax.nn.sigmoid(proj(off + d_model, d_model) + bg_ref[:, d_model:]).astype(BF16)


def _rope_tables(seq):
    rows = seq // GRID_W
    row_id = jnp.repeat(jnp.arange(rows, dtype=F32), GRID_W)
    col_id = jnp.tile(jnp.arange(GRID_W, dtype=F32), rows)
    half = HEAD_DIM // 2
    inv_freq = 1.0 / (ROPE_THETA ** (jnp.arange(0, half, 2, dtype=F32) / half))
    ang_r = row_id[:, None] * inv_freq[None, :]
    ang_c = col_id[:, None] * inv_freq[None, :]
    cos = jnp.concatenate([jnp.cos(ang_r)] * 2 + [jnp.cos(ang_c)] * 2, axis=1)
    sin = jnp.concatenate([-jnp.sin(ang_r), jnp.sin(ang_r), -jnp.sin(ang_c), jnp.sin(ang_c)], axis=1)
    return jnp.tile(cos, (1, 2)), jnp.tile(sin, (1, 2))


def _in_proj(x, norm1_g, w_in, b_gates, q_norm_g, k_norm_g, tm):
    bsz, seq, d_model = x.shape
    cos, sin = _rope_tables(seq)
    head_id = np.arange(A_Q_W) // HEAD_DIM
    hs = jnp.asarray(head_id[:, None] == head_id[None, :], BF16)
    qg = jnp.tile(q_norm_g, A_Q_W // HEAD_DIM)[None, :]
    kg = jnp.tile(k_norm_g, A_KV_W // HEAD_DIM)[None, :]
    row = lambda w: pl.BlockSpec((1, tm, w), lambda b, i: (b, i, 0))
    full = lambda a: pl.BlockSpec(a.shape, lambda b, i: (0,) * a.ndim)
    tab = pl.BlockSpec((tm, LANES), lambda b, i: (i, 0))
    g1 = norm1_g[None, :]
    bg = b_gates[None, :]
    w = w_in.astype(BF16)
    shp = lambda wd: jax.ShapeDtypeStruct((bsz, seq, wd), BF16)
    b_specs, b_shapes = [], []
    for _, dil in B_GROUPS:
        assert tm % (16 * dil) == 0
        b_specs += [pl.BlockSpec((1, dil, tm // dil, B_OUT_W), lambda b, i: (b, 0, i, 0))] * 3
        b_shapes += [jax.ShapeDtypeStruct((bsz, dil, seq // dil, B_OUT_W), BF16)] * 3
    outs = pl.pallas_call(
        functools.partial(_in_proj_body, d_model=d_model),
        grid=(bsz, seq // tm),
        in_specs=[row(d_model), full(g1), full(w), full(bg), full(qg), full(kg), tab, tab, full(hs)],
        out_specs=[row(A_Q_W), row(A_KV_W), row(2 * A_KV_W), row(d_model), row(d_model)] + b_specs,
        out_shape=[shp(A_Q_W), shp(A_KV_W), shp(2 * A_KV_W), shp(d_model), shp(d_model)] + b_shapes,
        scratch_shapes=[pltpu.VMEM((B_W // LANES, tm, LANES), F32)],
        compiler_params=pltpu.CompilerParams(
            dimension_semantics=("parallel", "parallel"), vmem_limit_bytes=VMEM_LIMIT),
        name="in_proj",
    )(x, g1, w, bg, qg, kg, cos, sin, hs)
    qa, ka, va, gate_a, gate_b = outs[:5]
    qkv_b = [outs[5 + 3 * gi:8 + 3 * gi] for gi in range(len(B_GROUPS))]
    return qa, ka, va, gate_a, gate_b, qkv_b


def _attn_a_body(q_ref, k_ref, v_ref, o_ref, qp_scr, s_scr, p_scr, m_scr, a_scr, acc_scr, *, tk):
    tq = q_ref.shape[1]
    seq = k_ref.shape[1]
    nk = seq // tk
    rows_all = A_Q_HEADS * tq
    q = q_ref[0].astype(F32)
    zeros = jnp.zeros((tq, HEAD_DIM), F32)
    for g in range(A_Q_HEADS):
        qg = q[:, g * HEAD_DIM:(g + 1) * HEAD_DIM]
        qp = jnp.concatenate([qg, zeros] if g // A_GROUP == 0 else [zeros, qg], axis=1)
        qp_scr[g * tq:(g + 1) * tq, :] = qp.astype(BF16)
    m_scr[...] = jnp.full_like(m_scr, NEG_INF)
    acc_scr[...] = jnp.zeros_like(acc_scr)

    def scores(kt, slot):
        start = pl.multiple_of(kt * tk, tk)
        s_scr[slot] = _nt_dot(qp_scr[...], k_ref[0, pl.ds(start, tk), :]).astype(BF16)

    def softmax_pv(kt, slot, next_kt=None, next_slot=None):
        if next_kt is not None:
            scores(next_kt, next_slot)
        for c in range(rows_all // ATTN_A_CHUNK):
            rows = slice(c * ATTN_A_CHUNK, (c + 1) * ATTN_A_CHUNK)
            s = s_scr[slot, rows, :]
            blocks = [s[:, j * LANES:(j + 1) * LANES] for j in range(tk // LANES)]
            while len(blocks) > 1:
                blocks = [jnp.maximum(a, b) for a, b in zip(blocks[::2], blocks[1::2])]
            m_prev = m_scr[rows, :]
            m_new = jnp.maximum(m_prev, jnp.max(blocks[0].astype(F32), axis=1, keepdims=True))
            m_scr[rows, :] = m_new
            a_scr[slot, rows, :] = jnp.exp2(m_prev - m_new)
            p_scr[slot, rows, :] = jnp.exp2(s - jnp.concatenate([m_new.astype(BF16)] * (tk // LANES), axis=1))
        start = pl.multiple_of(kt * tk, tk)
        pv = _dot(p_scr[slot], v_ref[0, pl.ds(start, tk), :])
        alpha = a_scr[slot]
        acc_scr[...] = jnp.concatenate([alpha, alpha], axis=1) * acc_scr[...] + pv

    scores(0, 0)
    for kt in range(nk):
        if kt + 1 < nk:
            softmax_pv(kt, kt % 2, kt + 1, (kt + 1) % 2)
        else:
            softmax_pv(kt, kt % 2)

    outs = []
    for g in range(A_Q_HEADS):
        kv = g // A_GROUP
        rows = slice(g * tq, (g + 1) * tq)
        outs.append(acc_scr[rows, kv * HEAD_DIM:(kv + 1) * HEAD_DIM]
                    / acc_scr[rows, A_KV_W:A_KV_W + HEAD_DIM])
    o_ref[0] = jnp.concatenate(outs, axis=1).astype(o_ref.dtype)


ATTN_A_CHUNK = 128


def _attn_a(qa, ka, va, tq, tk):
    bsz, seq, _ = qa.shape
    rows_all = A_Q_HEADS * tq
    return pl.pallas_call(
        functools.partial(_attn_a_body, tk=tk),
        grid=(bsz, seq // tq),
        in_specs=[pl.BlockSpec((1, tq, A_Q_W), lambda b, i: (b, i, 0)),
                  pl.BlockSpec((1, seq, A_KV_W), lambda b, i: (b, 0, 0)),
                  pl.BlockSpec((1, seq, 2 * A_KV_W), lambda b, i: (b, 0, 0))],
        out_specs=pl.BlockSpec((1, tq, A_Q_W), lambda b, i: (b, i, 0)),
        out_shape=jax.ShapeDtypeStruct((bsz, seq, A_Q_W), BF16),
        scratch_shapes=[pltpu.VMEM((rows_all, A_KV_W), BF16),
                        pltpu.VMEM((2, rows_all, tk), BF16),
                        pltpu.VMEM((2, rows_all, tk), BF16),
                        pltpu.VMEM((rows_all, LANES), F32),
                        pltpu.VMEM((2, rows_all, LANES), F32),
                        pltpu.VMEM((rows_all, 2 * A_KV_W), F32)],
        compiler_params=pltpu.CompilerParams(
            dimension_semantics=("parallel", "parallel"), vmem_limit_bytes=VMEM_LIMIT),
        name="attn_a",
    )(qa, ka, va)


HALO = 128
SUB = 128
N_SIDE = 64
WIN = SUB + 2 * N_SIDE


def _attn_b_body(q_ref, kp_ref, kc_ref, kn_ref, vp_ref, vc_ref, vn_ref, bias_ref, o_ref, lse_ref,
                 kbuf, vbuf, *, seq_d):
    tu = q_ref.shape[2]
    i = pl.program_id(2)
    kbuf[0:HALO] = kp_ref[0, 0]
    kbuf[HALO:HALO + tu] = kc_ref[0, 0]
    kbuf[HALO + tu:] = kn_ref[0, 0]
    vbuf[0:HALO] = vp_ref[0, 0]
    vbuf[HALO:HALO + tu] = vc_ref[0, 0]
    vbuf[HALO + tu:] = vn_ref[0, 0]

    nh = B_HEADS_PER_GROUP
    nsub = tu // SUB
    head_of_lane = lax.broadcasted_iota(jnp.int32, (1, B_OUT_W), 1) // HEAD_DIM
    col = lax.broadcasted_iota(jnp.int32, (1, WIN), 1)

    def scores(j):
        qs = q_ref[0, 0, j * SUB:(j + 1) * SUB, :]
        qst = jnp.concatenate([jnp.where(head_of_lane == hh, qs, jnp.zeros_like(qs)) for hh in range(nh)],
                              axis=0)
        w0 = HALO - N_SIDE + j * SUB
        return _nt_dot(qst, kbuf[w0:w0 + WIN, :])

    s_next = scores(0)
    for j in range(nsub):
        s = s_next + bias_ref[...]
        if j + 1 < nsub:
            s_next = scores(j + 1)
        if j == 0 or j == nsub - 1:
            key_u = i * tu + j * SUB - N_SIDE + col
            s = jnp.where((key_u >= 0) & (key_u < seq_d), s, NEG_INF)
        m = jnp.max(s, axis=1, keepdims=True)
        p = jnp.exp(s - m)
        l = jnp.sum(p, axis=1, keepdims=True)
        w0 = HALO - N_SIDE + j * SUB
        pv = _dot(p.astype(BF16), vbuf[w0:w0 + WIN, :])
        on = pv / l
        lse = m + jnp.log(l)
        o_acc = jnp.zeros((SUB, B_OUT_W), F32)
        lse_acc = jnp.zeros((SUB, B_OUT_W), F32)
        for hh in range(nh):
            rows = slice(hh * SUB, (hh + 1) * SUB)
            mine = head_of_lane == hh
            o_acc = jnp.where(mine, on[rows], o_acc)
            lse_acc = jnp.where(mine, lse[rows], lse_acc)
        o_ref[0, 0, j * SUB:(j + 1) * SUB, :] = o_acc
        lse_ref[0, 0, j * SUB:(j + 1) * SUB, :] = lse_acc


def _band_bias(group, dil):
    row = np.arange(SUB)[:, None]
    col = np.arange(WIN)[None, :]
    rel = col - N_SIDE - row
    tiles = []
    for hh in range(B_HEADS_PER_GROUP):
        slope = 2.0 ** (-8.0 * (group * B_HEADS_PER_GROUP + hh + 1) / B_HEADS)
        tiles.append(np.where(np.abs(rel) <= N_SIDE, -slope * np.abs(rel) * dil, NEG_INF))
    return jnp.asarray(np.concatenate(tiles, axis=0), F32)


def _attn_b_group(q, k, v, group, dil, tu):
    bsz, _, seq_d, _ = q.shape
    tu = min(tu, seq_d)
    per = tu // HALO
    last = seq_d // HALO - 1
    bias = _band_bias(group, dil)
    cur = pl.BlockSpec((1, 1, tu, B_OUT_W), lambda b, r, i: (b, r, i, 0))
    prev = pl.BlockSpec((1, 1, HALO, B_OUT_W), lambda b, r, i: (b, r, jnp.maximum(i * per - 1, 0), 0))
    nxt = pl.BlockSpec((1, 1, HALO, B_OUT_W), lambda b, r, i: (b, r, jnp.minimum((i + 1) * per, last), 0))
    return pl.pallas_call(
        functools.partial(_attn_b_body, seq_d=seq_d),
        grid=(bsz, dil, seq_d // tu),
        in_specs=[cur, prev, cur, nxt, prev, cur, nxt,
                  pl.BlockSpec(bias.shape, lambda b, r, i: (0, 0))],
        out_specs=[cur, cur],
        out_shape=[jax.ShapeDtypeStruct((bsz, dil, seq_d, B_OUT_W), F32)] * 2,
        scratch_shapes=[pltpu.VMEM((tu + 2 * HALO, B_OUT_W), BF16)] * 2,
        compiler_params=pltpu.CompilerParams(
            dimension_semantics=("parallel", "parallel", "parallel"), vmem_limit_bytes=VMEM_LIMIT),
        name=f"attn_b{group}",
    )(q, k, k, k, v, v, v, bias)


def _merge_body(x_ref, oa_ref, o0_ref, o1_ref, o2_ref, l0_ref, l1_ref, l2_ref, ga_ref, gb_ref,
                woa_ref, wob_ref, wout_ref, g2_ref, wrh_ref, wrl_ref,
                x2_ref, h2_ref, aff_ref, o_scr, l_scr):
    tm = x_ref.shape[1]
    slabs = B_OUT_W // LANES
    for gi, (o_ref, lse_ref) in enumerate(((o0_ref, l0_ref), (o1_ref, l1_ref), (o2_ref, l2_ref))):
        dil = B_GROUPS[gi][1]
        for r in range(dil):
            for c in range(slabs):
                lanes = slice(c * LANES, (c + 1) * LANES)
                o_scr[gi * slabs + c, pl.ds(r, tm // dil, stride=dil), :] = o_ref[0, r, :, lanes]
                l_scr[gi * slabs + c, pl.ds(r, tm // dil, stride=dil), :] = lse_ref[0, r, :, lanes]
    whole = lambda scr, gi: jnp.concatenate([scr[gi * slabs + c] for c in range(slabs)], axis=1)
    l0, l1, l2 = whole(l_scr, 0), whole(l_scr, 1), whole(l_scr, 2)
    mx = jnp.maximum(jnp.maximum(l0, l1), l2)
    e0, e1, e2 = jnp.exp(l0 - mx), jnp.exp(l1 - mx), jnp.exp(l2 - mx)
    ob = (e0 * whole(o_scr, 0) + e1 * whole(o_scr, 1) + e2 * whole(o_scr, 2)) / (e0 + e1 + e2)
    ya = _dot(oa_ref[0], woa_ref[...])
    yb = _dot(ob.astype(BF16), wob_ref[...])
    mrg = ga_ref[0].astype(F32) * ya + gb_ref[0].astype(F32) * yb
    x2 = x_ref[0] + _dot(mrg.astype(BF16), wout_ref[...])
    x2_ref[0] = x2
    ms = jnp.mean(x2 * x2, axis=-1, keepdims=True)
    h2 = x2 * lax.rsqrt(ms + NORM_EPS) * g2_ref[...]
    h2_ref[0] = h2.astype(BF16)
    hh, hl = _split_bf16(h2)
    wh, wl = wrh_ref[...], wrl_ref[...]
    logits = _nt_dot(wh, hh) + _nt_dot(wh, hl) + _nt_dot(wl, hh)
    mxl = jnp.max(logits, axis=0, keepdims=True)
    ex = jnp.exp(logits - mxl)
    aff_ref[0] = ex / jnp.sum(ex, axis=0, keepdims=True)


def _merge(x, oa, obs, lses, gate_a, gate_b, w_o_a, w_o_b, w_out, norm2_g, w_router, tm):
    bsz, seq, d_model = x.shape
    row = lambda w: pl.BlockSpec((1, tm, w), lambda b, i: (b, i, 0))
    full = lambda a: pl.BlockSpec(a.shape, lambda b, i: (0,) * a.ndim)
    res = [pl.BlockSpec((1, dil, tm // dil, B_OUT_W), lambda b, i: (b, 0, i, 0)) for _, dil in B_GROUPS]
    woa, wob, wout = w_o_a.astype(BF16), w_o_b.astype(BF16), w_out.astype(BF16)
    g2 = norm2_g[None, :]
    wrh, wrl = _split_bf16(w_router.T)
    return pl.pallas_call(
        _merge_body,
        grid=(bsz, seq // tm),
        in_specs=[row(d_model), row(A_Q_W)] + res + res + [row(d_model)] * 2
                 + [full(woa), full(wob), full(wout), full(g2), full(wrh), full(wrl)],
        out_specs=[row(d_model), row(d_model),
                   pl.BlockSpec((1, N_EXPERTS, tm), lambda b, i: (b, 0, i))],
        out_shape=[jax.ShapeDtypeStruct((bsz, seq, d_model), F32),
                   jax.ShapeDtypeStruct((bsz, seq, d_model), BF16),
                   jax.ShapeDtypeStruct((bsz, N_EXPERTS, seq), F32)],
        scratch_shapes=[pltpu.VMEM((len(B_GROUPS) * B_OUT_W // LANES, tm, LANES), F32)] * 2,
        compiler_params=pltpu.CompilerParams(
            dimension_semantics=("parallel", "parallel"), vmem_limit_bytes=VMEM_LIMIT),
        name="merge",
    )(x, oa, *obs, *lses, gate_a, gate_b, woa, wob, wout, g2, wrh, wrl)


TOK_BLK = 256
GATHER_ROWS = 128
META_LO = 64
META_HI = 96


def _route_body(aff_ref, pos_ref, post_ref, gt_ref, meta_ref, *, cap):
    aff = aff_ref[0]
    n_e, seq = aff.shape
    bits = pltpu.bitcast(aff, jnp.int32)

    def refine(it, thr):
        cand = thr | jnp.left_shift(jnp.int32(1), 30 - it)
        cnt = jnp.sum(jnp.where(bits >= cand, 1.0, 0.0), axis=1, keepdims=True)
        return jnp.where(cnt >= cap, cand, thr)

    thr = lax.fori_loop(0, 31, refine, jnp.zeros((n_e, 1), jnp.int32))
    above = bits > thr
    tied = bits == thr
    need = cap - jnp.sum(jnp.where(above, 1.0, 0.0), axis=1, keepdims=True)

    ri = lax.broadcasted_iota(jnp.int32, (TOK_BLK, TOK_BLK), 0)
    ci = lax.broadcasted_iota(jnp.int32, (TOK_BLK, TOK_BLK), 1)
    before = jnp.where(ri < ci, 1.0, 0.0).astype(BF16)
    lane = lax.broadcasted_iota(jnp.int32, (1, LANES), 1)
    pad_rows = LANES - n_e

    run_tied = jnp.zeros((n_e, 1), F32)
    run_sel = jnp.zeros((n_e, 1), F32)
    starts = jnp.zeros((n_e, LANES), F32)
    ends = jnp.zeros((n_e, LANES), F32)
    n_blk = seq // TOK_BLK
    for c in range(n_blk):
        sl = slice(c * TOK_BLK, (c + 1) * TOK_BLK)
        tied_c = jnp.where(tied[:, sl], 1.0, 0.0)
        rank = run_tied + _dot(tied_c.astype(BF16), before)
        sel = above[:, sl] | (tied[:, sl] & (rank < need))
        sel_c = jnp.where(sel, 1.0, 0.0)
        pos = run_sel + _dot(sel_c.astype(BF16), before)
        pos_m = jnp.where(sel, pos, -1.0)
        g_m = jnp.where(sel, aff[:, sl], 0.0)
        pos_ref[0, :, sl] = pos_m
        fill = jnp.full((pad_rows, TOK_BLK), -1.0, F32)
        post_ref[0, sl, :] = jnp.concatenate([pos_m, fill], axis=0).T
        gt_ref[0, sl, :] = jnp.concatenate([g_m, jnp.zeros((pad_rows, TOK_BLK), F32)], axis=0).T
        starts = jnp.where(lane == c, run_sel, starts)
        run_tied = run_tied + jnp.sum(tied_c, axis=1, keepdims=True)
        run_sel = run_sel + jnp.sum(sel_c, axis=1, keepdims=True)
        ends = jnp.where(lane == c, run_sel, ends)
    meta = jnp.where(lane == n_blk, run_sel, starts)
    real = lane < n_blk
    for j in range(cap // GATHER_ROWS):
        c0 = float(j * GATHER_ROWS)
        first = jnp.sum(jnp.where(real & (ends <= c0), 1.0, 0.0), axis=1, keepdims=True)
        stop = n_blk - jnp.sum(jnp.where(real & (starts >= c0 + GATHER_ROWS), 1.0, 0.0), axis=1, keepdims=True)
        meta = jnp.where(lane == META_LO + j, first, meta)
        meta = jnp.where(lane == META_HI + j, stop, meta)
    meta_ref[0] = meta.astype(jnp.int32)


def _route(aff, cap):
    bsz, n_e, seq = aff.shape
    return pl.pallas_call(
        functools.partial(_route_body, cap=cap),
        grid=(bsz,),
        in_specs=[pl.BlockSpec((1, n_e, seq), lambda b: (b, 0, 0))],
        out_specs=[pl.BlockSpec((1, n_e, seq), lambda b: (b, 0, 0)),
                   pl.BlockSpec((1, seq, LANES), lambda b: (b, 0, 0)),
                   pl.BlockSpec((1, seq, LANES), lambda b: (b, 0, 0)),
                   pl.BlockSpec((1, n_e, LANES), lambda b: (b, 0, 0))],
        out_shape=[jax.ShapeDtypeStruct((bsz, n_e, seq), F32),
                   jax.ShapeDtypeStruct((bsz, seq, LANES), F32),
                   jax.ShapeDtypeStruct((bsz, seq, LANES), F32),
                   jax.ShapeDtypeStruct((bsz, n_e, LANES), jnp.int32)],
        compiler_params=pltpu.CompilerParams(
            dimension_semantics=("parallel",), vmem_limit_bytes=VMEM_LIMIT),
        name="route",
    )(aff)


GATHER_SPAN = 6


def _gather_body(meta_ref, h_ref, pos_ref, xin_ref, acc_ref, *, cap):
    b, e = pl.program_id(0), pl.program_id(1)
    base = (b * N_EXPERTS + e) * LANES
    n_blk = pos_ref.shape[2]
    half = GATHER_SPAN // 2
    slot = lax.broadcasted_iota(jnp.int32, (GATHER_ROWS, 1), 0)
    for j in range(cap // GATHER_ROWS):
        want = (slot + j * GATHER_ROWS).astype(F32)
        acc_ref[...] = jnp.zeros_like(acc_ref)
        first = meta_ref[base + META_LO + j]
        stop = meta_ref[base + META_HI + j]

        def add_blocks(i, carry, want=want, first=first):
            begin = first + GATHER_SPAN * i
            begin_c = jnp.minimum(begin, n_blk - GATHER_SPAN)
            for c in range(2):
                blk0 = begin_c + c * half
                pieces = []
                for t in range(half):
                    pos = pos_ref[0, 0, pl.ds(blk0 + t, 1), :]
                    pos = jnp.where(blk0 + t >= begin, pos, -1.0)
                    pieces.append(jnp.where(pos == want, 1.0, 0.0))
                onehot = jnp.concatenate(pieces, axis=1).astype(BF16)
                hblk = h_ref[0, pl.ds(pl.multiple_of(blk0 * TOK_BLK, TOK_BLK), half * TOK_BLK), :]
                acc_ref[c] += _dot(onehot, hblk)
            return carry

        lax.fori_loop(0, (stop - first + GATHER_SPAN - 1) // GATHER_SPAN, add_blocks, 0)
        xin_ref[0, 0, j * GATHER_ROWS:(j + 1) * GATHER_ROWS, :] = (acc_ref[0] + acc_ref[1]).astype(xin_ref.dtype)


def _gather(meta_flat, h2, pos, cap):
    bsz, seq, d_model = h2.shape
    n_e = pos.shape[1]
    assert seq // TOK_BLK >= GATHER_SPAN
    pos4 = pos.reshape(bsz, n_e, seq // TOK_BLK, TOK_BLK)
    return pl.pallas_call(
        functools.partial(_gather_body, cap=cap),
        grid_spec=pltpu.PrefetchScalarGridSpec(
            num_scalar_prefetch=1,
            grid=(bsz, n_e),
            in_specs=[pl.BlockSpec((1, seq, d_model), lambda b, e, m: (b, 0, 0)),
                      pl.BlockSpec((1, 1, seq // TOK_BLK, TOK_BLK), lambda b, e, m: (b, e, 0, 0))],
            out_specs=pl.BlockSpec((1, 1, cap, d_model), lambda b, e, m: (b, e, 0, 0)),
            scratch_shapes=[pltpu.VMEM((2, GATHER_ROWS, d_model), F32)]),
        out_shape=jax.ShapeDtypeStruct((bsz, n_e, cap, d_model), BF16),
        compiler_params=pltpu.CompilerParams(
            dimension_semantics=("parallel", "parallel"), vmem_limit_bytes=VMEM_LIMIT),
        name="gather",
    )(meta_flat, h2, pos4)


FFN_ROWS = 256


def _ffn_body(x_ref, wg_ref, wu_ref, wd_ref, y_ref):
    cap = x_ref.shape[2]
    for r in range(cap // FFN_ROWS):
        rows = slice(r * FFN_ROWS, (r + 1) * FFN_ROWS)
        xin = x_ref[0, 0, rows, :]
        a = _dot(xin, wg_ref[0])
        u = _dot(xin, wu_ref[0])
        mid = (a * jax.nn.sigmoid(a) * u).astype(BF16)
        y_ref[0, 0, rows, :] = _dot(mid, wd_ref[0]).astype(y_ref.dtype)


def _ffn(xin, w_gate, w_up, w_down):
    bsz, n_e, cap, d_model = xin.shape
    ff = w_gate.shape[2]
    tok = pl.BlockSpec((1, 1, cap, d_model), lambda e, b: (b, e, 0, 0))
    return pl.pallas_call(
        _ffn_body,
        grid=(n_e, bsz),
        in_specs=[tok,
                  pl.BlockSpec((1, d_model, ff), lambda e, b: (e, 0, 0)),
                  pl.BlockSpec((1, d_model, ff), lambda e, b: (e, 0, 0)),
                  pl.BlockSpec((1, ff, d_model), lambda e, b: (e, 0, 0))],
        out_specs=tok,
        out_shape=jax.ShapeDtypeStruct((bsz, n_e, cap, d_model), BF16),
        compiler_params=pltpu.CompilerParams(
            dimension_semantics=("parallel", "parallel"), vmem_limit_bytes=VMEM_LIMIT),
        name="ffn",
    )(xin, w_gate.astype(BF16), w_up.astype(BF16), w_down.astype(BF16))


SCATTER_WIN = 128
SCATTER_ALIGN = 64
SCATTER_EXPERTS = 4


def _scatter_body(meta_ref, y_ref, post_ref, gt_ref, x2_ref, gf_ref, o_ref, *, tc):
    b, t, eg = pl.program_id(0), pl.program_id(1), pl.program_id(2)
    cap = y_ref.shape[2]
    n_blk = tc // TOK_BLK

    @pl.when(eg == 0)
    def _():
        o_ref[0] = x2_ref[0]

    lane = lax.broadcasted_iota(jnp.int32, (1, LANES), 1)
    slot = lax.broadcasted_iota(jnp.int32, (1, SCATTER_WIN), 1)

    def expert_range(kk, el):
        base = (b * N_EXPERTS + eg * SCATTER_EXPERTS + el) * LANES + t * n_blk + kk
        first = meta_ref[base]
        last = meta_ref[base + 1]
        start = jnp.minimum((first // SCATTER_ALIGN) * SCATTER_ALIGN, cap - SCATTER_WIN)
        return start, last

    def columns(kk, el):
        rows = slice(kk * TOK_BLK, (kk + 1) * TOK_BLK)
        mine = lane == eg * SCATTER_EXPERTS + el
        pcol = jnp.sum(jnp.where(mine, post_ref[0, rows, :], 0.0), axis=1, keepdims=True)
        gcol = jnp.sum(jnp.where(mine, gt_ref[0, rows, :], 0.0), axis=1, keepdims=True)
        return pcol, gcol

    def window(el, start):
        return y_ref[0, el, pl.ds(pl.multiple_of(start, SCATTER_ALIGN), SCATTER_WIN), :]

    for kk in range(n_blk):
        rows = slice(kk * TOK_BLK, (kk + 1) * TOK_BLK)
        for pair in range(SCATTER_EXPERTS // 2):
            sel, ywin = [], []
            for el in (2 * pair, 2 * pair + 1):
                start, _ = expert_range(kk, el)
                pcol, gcol = columns(kk, el)
                sel.append(jnp.where(pcol == (slot + start).astype(F32), gcol, 0.0))
                ywin.append(window(el, start))
            o_ref[0, rows, :] += _dot(jnp.concatenate(sel, axis=1).astype(BF16), jnp.concatenate(ywin, axis=0))

    for kk in range(n_blk):
        rows = slice(kk * TOK_BLK, (kk + 1) * TOK_BLK)
        for el in range(SCATTER_EXPERTS):
            start, last = expert_range(kk, el)

            @pl.when(last > start + SCATTER_WIN)
            def _(kk=kk, el=el, rows=rows, start=start, last=last):
                pcol, gcol = columns(kk, el)

                def more(j, carry):
                    begin = start + (j + 1) * SCATTER_WIN
                    begin_c = jnp.minimum(begin, cap - SCATTER_WIN)
                    want = slot + begin_c
                    sel = jnp.where((pcol == want.astype(F32)) & (want >= begin), gcol, 0.0)
                    o_ref[0, rows, :] += _dot(sel.astype(BF16), window(el, begin_c))
                    return carry

                lax.fori_loop(0, (last - start - 1) // SCATTER_WIN, more, 0)

    @pl.when(eg == pl.num_programs(2) - 1)
    def _():
        v = o_ref[0]
        ms = jnp.mean(v * v, axis=-1, keepdims=True)
        o_ref[0] = v * lax.rsqrt(ms + NORM_EPS) * gf_ref[...]


def _scatter(meta_flat, y, post, gt, x2, normf_g, tc):
    bsz, n_e, cap, d_model = y.shape
    seq = x2.shape[1]
    tc = min(tc, seq)
    gf = normf_g[None, :]
    assert n_e % SCATTER_EXPERTS == 0 and cap % SCATTER_ALIGN == 0 and cap >= SCATTER_WIN
    tokw = lambda w, **kw: pl.BlockSpec((1, tc, w), lambda b, t, e, m: (b, t, 0), **kw)
    return pl.pallas_call(
        functools.partial(_scatter_body, tc=tc),
        grid_spec=pltpu.PrefetchScalarGridSpec(
            num_scalar_prefetch=1,
            grid=(bsz, seq // tc, n_e // SCATTER_EXPERTS),
            in_specs=[pl.BlockSpec((1, SCATTER_EXPERTS, cap, d_model), lambda b, t, e, m: (b, e, 0, 0)),
                      tokw(LANES), tokw(LANES),
                      tokw(d_model, pipeline_mode=pl.Buffered(1)),
                      pl.BlockSpec(gf.shape, lambda b, t, e, m: (0, 0))],
            out_specs=tokw(d_model)),
        out_shape=jax.ShapeDtypeStruct((bsz, seq, d_model), F32),
        compiler_params=pltpu.CompilerParams(
            dimension_semantics=("parallel", "parallel", "arbitrary"), vmem_limit_bytes=VMEM_LIMIT),
        name="scatter",
    )(meta_flat, y, post, gt, x2, gf)


def kernel(x, norm1_g, w_in, b_gates, q_norm_g, k_norm_g, w_o_a, w_o_b, w_out, norm2_g, w_router,
           w_gate, w_up, w_down, normf_g):
    bsz, seq, d_model = x.shape
    depth = norm1_g.shape[0]
    cap = EC_CAPACITY_FACTOR * seq // N_EXPERTS
    tm = min(512, seq)
    for l in range(depth):
        qa, ka, va, gate_a, gate_b, qkv_b = _in_proj(
            x, norm1_g[l], w_in[l], b_gates[l], q_norm_g[l], k_norm_g[l], tm)
        oa = _attn_a(qa, ka, va, tq=min(128, seq), tk=min(1024, seq))
        obs, lses = [], []
        for gi, (_, dil) in enumerate(B_GROUPS):
            o, lse = _attn_b_group(*qkv_b[gi], gi, dil, tu=512)
            obs.append(o)
            lses.append(lse)
        x2, h2, aff = _merge(x, oa, obs, lses, gate_a, gate_b, w_o_a[l], w_o_b[l], w_out[l],
                             norm2_g[l], w_router[l], tm)
        pos, post, gt, meta = _route(aff, cap)
        meta_flat = meta.reshape(-1)
        xin = _gather(meta_flat, h2, pos, cap)
        y = _ffn(xin, w_gate[l], w_up[l], w_down[l])
        assert depth == 1
        x = _scatter(meta_flat, y, post, gt, x2, normf_g, tc=2048)
    return x
```

```python
import functools

import numpy as np
import jax
import jax.numpy as jnp
from jax import lax
from jax.experimental import pallas as pl
from jax.experimental.pallas import tpu as pltpu

HEAD_DIM = 64
A_Q_HEADS = 8
A_KV_HEADS = 2
A_GROUP = A_Q_HEADS // A_KV_HEADS
B_GROUPS = ((128, 1), (512, 4), (2048, 16))
B_HEADS_PER_GROUP = 4
B_HEADS = B_HEADS_PER_GROUP * len(B_GROUPS)
GRID_W = 64
ROPE_THETA = 10000.0
N_EXPERTS = 16
EC_CAPACITY_FACTOR = 2
NORM_EPS = 1e-6
NEG_INF = -1e30

A_Q_W = A_Q_HEADS * HEAD_DIM
A_KV_W = A_KV_HEADS * HEAD_DIM
B_W = B_HEADS * HEAD_DIM
B_OUT_W = B_HEADS_PER_GROUP * HEAD_DIM

LANES = 128
MXU_DIM = 256
VMEM_LIMIT = 56 * 1024 * 1024

F32 = jnp.float32
BF16 = jnp.bfloat16
LOG2_E = 1.4426950408889634


def _nt_dot(a, b):
    return lax.dot_general(a, b, (((1,), (1,)), ((), ())), preferred_element_type=F32)


def _dot(a, b):
    return jnp.dot(a, b, preferred_element_type=F32)


def _split_bf16(v):
    hi = v.astype(BF16)
    lo = (v - hi.astype(F32)).astype(BF16)
    return hi, lo


def _in_proj_body(x_ref, g1_ref, w_ref, bg_ref, qg_ref, kg_ref, cos_ref, sin_ref, hs_ref,
                  qa_ref, ka_ref, va_ref, ga_ref, gb_ref, *rest, d_model):
    b_refs, zb_scr = rest[:-1], rest[-1]
    tm = x_ref.shape[1]
    x = x_ref[0]
    ms = jnp.mean(x * x, axis=-1, keepdims=True)
    h = (x * lax.rsqrt(ms + NORM_EPS) * g1_ref[...]).astype(BF16)

    def proj(lo, width):
        return _dot(h, w_ref[:, lo:lo + width])

    cos = cos_ref[...]
    sin = sin_ref[...]
    lane = lax.broadcasted_iota(jnp.int32, (1, LANES), 1)
    low = (lane % 32) < 16

    def head_norm_rope(z, gain, nrep):
        width = LANES * nrep
        z2 = z * z
        hi, lo = _split_bf16(z2)
        hs = hs_ref[:width, :width]
        ss = _dot(hi, hs) + _dot(lo, hs)
        zn = z * lax.rsqrt(ss * (1.0 / HEAD_DIM) + NORM_EPS) * gain
        rep = lambda t: jnp.concatenate([t] * nrep, axis=1) if nrep > 1 else t
        partner = jnp.where(rep(low), pltpu.roll(zn, width - 16, 1), pltpu.roll(zn, 16, 1))
        return zn * rep(cos) + partner * rep(sin)

    scale = HEAD_DIM ** -0.5
    za = proj(0, A_Q_W + 2 * A_KV_W)
    qa = head_norm_rope(za[:, :A_Q_W], qg_ref[...], A_Q_W // LANES) * (scale * LOG2_E)
    ka = head_norm_rope(za[:, A_Q_W:A_Q_W + A_KV_W], kg_ref[...], A_KV_W // LANES)
    qa_ref[0] = qa.astype(BF16)
    ka_ref[0] = ka.astype(BF16)
    va = za[:, A_Q_W + A_KV_W:]
    va_ref[0] = jnp.concatenate([va, jnp.ones_like(va)], axis=1).astype(BF16)
    off = A_Q_W + 2 * A_KV_W
    slabs = B_OUT_W // LANES
    for t in range(3):
        z = proj(off + t * B_W, B_W) * (scale if t == 0 else 1.0)
        for c in range(B_W // LANES):
            zb_scr[c] = z[:, c * LANES:(c + 1) * LANES]
        for gi, (_, dil) in enumerate(B_GROUPS):
            out = b_refs[3 * gi + t]
            for r in range(dil):
                for c in range(slabs):
                    out[0, r, :, c * LANES:(c + 1) * LANES] = (
                        zb_scr[gi * slabs + c, pl.ds(r, tm // dil, stride=dil), :].astype(BF16))
    off += 3 * B_W
    ga_ref[0] = jax.nn.sigmoid(proj(off, d_model) + bg_ref[:, :d_model]).astype(BF16)
    gb_ref[0] = jax.nn.sigmoid(proj(off + d_model, d_model) + bg_ref[:, d_model:]).astype(BF16)


def _rope_tables(seq):
    rows = seq // GRID_W
    row_id = jnp.repeat(jnp.arange(rows, dtype=F32), GRID_W)
    col_id = jnp.tile(jnp.arange(GRID_W, dtype=F32), rows)
    half = HEAD_DIM // 2
    inv_freq = 1.0 / (ROPE_THETA ** (jnp.arange(0, half, 2, dtype=F32) / half))
    ang_r = row_id[:, None] * inv_freq[None, :]
    ang_c = col_id[:, None] * inv_freq[None, :]
    cos = jnp.concatenate([jnp.cos(ang_r)] * 2 + [jnp.cos(ang_c)] * 2, axis=1)
    sin = jnp.concatenate([-jnp.sin(ang_r), jnp.sin(ang_r), -jnp.sin(ang_c), jnp.sin(ang_c)], axis=1)
    return jnp.tile(cos, (1, 2)), jnp.tile(sin, (1, 2))


def _in_proj(x, norm1_g, w_in, b_gates, q_norm_g, k_norm_g, tm):
    bsz, seq, d_model = x.shape
    cos, sin = _rope_tables(seq)
    head_id = np.arange(A_Q_W) // HEAD_DIM
    hs = jnp.asarray(head_id[:, None] == head_id[None, :], BF16)
    qg = jnp.tile(q_norm_g, A_Q_W // HEAD_DIM)[None, :]
    kg = jnp.tile(k_norm_g, A_KV_W // HEAD_DIM)[None, :]
    row = lambda w: pl.BlockSpec((1, tm, w), lambda b, i: (b, i, 0))
    full = lambda a: pl.BlockSpec(a.shape, lambda b, i: (0,) * a.ndim)
    tab = pl.BlockSpec((tm, LANES), lambda b, i: (i, 0))
    g1 = norm1_g[None, :]
    bg = b_gates[None, :]
    w = w_in.astype(BF16)
    shp = lambda wd: jax.ShapeDtypeStruct((bsz, seq, wd), BF16)
    b_specs, b_shapes = [], []
    for _, dil in B_GROUPS:
        assert tm % (16 * dil) == 0
        b_specs += [pl.BlockSpec((1, dil, tm // dil, B_OUT_W), lambda b, i: (b, 0, i, 0))] * 3
        b_shapes += [jax.ShapeDtypeStruct((bsz, dil, seq // dil, B_OUT_W), BF16)] * 3
    outs = pl.pallas_call(
        functools.partial(_in_proj_body, d_model=d_model),
        grid=(bsz, seq // tm),
        in_specs=[row(d_model), full(g1), full(w), full(bg), full(qg), full(kg), tab, tab, full(hs)],
        out_specs=[row(A_Q_W), row(A_KV_W), row(2 * A_KV_W), row(d_model), row(d_model)] + b_specs,
        out_shape=[shp(A_Q_W), shp(A_KV_W), shp(2 * A_KV_W), shp(d_model), shp(d_model)] + b_shapes,
        scratch_shapes=[pltpu.VMEM((B_W // LANES, tm, LANES), F32)],
        compiler_params=pltpu.CompilerParams(
            dimension_semantics=("parallel", "parallel"), vmem_limit_bytes=VMEM_LIMIT),
        name="in_proj",
    )(x, g1, w, bg, qg, kg, cos, sin, hs)
    qa, ka, va, gate_a, gate_b = outs[:5]
    qkv_b = [outs[5 + 3 * gi:8 + 3 * gi] for gi in range(len(B_GROUPS))]
    return qa, ka, va, gate_a, gate_b, qkv_b


def _attn_a_body(q_ref, k_ref, v_ref, o_ref, qp_scr, s_scr, p_scr, m_scr, a_scr, acc_scr, *, tk):
    tq = q_ref.shape[1]
    seq = k_ref.shape[1]
    nk = seq // tk
    rows_all = A_Q_HEADS * tq
    q = q_ref[0].astype(F32)
    zeros = jnp.zeros((tq, HEAD_DIM), F32)
    for g in range(A_Q_HEADS):
        qg = q[:, g * HEAD_DIM:(g + 1) * HEAD_DIM]
        qp = jnp.concatenate([qg, zeros] if g // A_GROUP == 0 else [zeros, qg], axis=1)
        qp_scr[g * tq:(g + 1) * tq, :] = qp.astype(BF16)
    m_scr[...] = jnp.full_like(m_scr, NEG_INF)
    acc_scr[...] = jnp.zeros_like(acc_scr)

    def scores(kt, slot):
        start = pl.multiple_of(kt * tk, tk)
        s_scr[slot] = _nt_dot(qp_scr[...], k_ref[0, pl.ds(start, tk), :]).astype(BF16)

    def softmax_pv(kt, slot, next_kt=None, next_slot=None):
        if next_kt is not None:
            scores(next_kt, next_slot)
        for c in range(rows_all // ATTN_A_CHUNK):
            rows = slice(c * ATTN_A_CHUNK, (c + 1) * ATTN_A_CHUNK)
            s = s_scr[slot, rows, :]
            blocks = [s[:, j * LANES:(j + 1) * LANES] for j in range(tk // LANES)]
            while len(blocks) > 1:
                blocks = [jnp.maximum(a, b) for a, b in zip(blocks[::2], blocks[1::2])]
            m_prev = m_scr[rows, :]
            m_new = jnp.maximum(m_prev, jnp.max(blocks[0].astype(F32), axis=1, keepdims=True))
            m_scr[rows, :] = m_new
            a_scr[slot, rows, :] = jnp.exp2(m_prev - m_new)
            p_scr[slot, rows, :] = jnp.exp2(s - jnp.concatenate([m_new.astype(BF16)] * (tk // LANES), axis=1))
        start = pl.multiple_of(kt * tk, tk)
        pv = _dot(p_scr[slot], v_ref[0, pl.ds(start, tk), :])
        alpha = a_scr[slot]
        acc_scr[...] = jnp.concatenate([alpha, alpha], axis=1) * acc_scr[...] + pv

    scores(0, 0)
    for kt in range(nk):
        if kt + 1 < nk:
            softmax_pv(kt, kt % 2, kt + 1, (kt + 1) % 2)
        else:
            softmax_pv(kt, kt % 2)

    outs = []
    for g in range(A_Q_HEADS):
        kv = g // A_GROUP
        rows = slice(g * tq, (g + 1) * tq)
        outs.append(acc_scr[rows, kv * HEAD_DIM:(kv + 1) * HEAD_DIM]
                    / acc_scr[rows, A_KV_W:A_KV_W + HEAD_DIM])
    o_ref[0] = jnp.concatenate(outs, axis=1).astype(o_ref.dtype)


ATTN_A_CHUNK = 128


def _attn_a(qa, ka, va, tq, tk):
    bsz, seq, _ = qa.shape
    rows_all = A_Q_HEADS * tq
    return pl.pallas_call(
        functools.partial(_attn_a_body, tk=tk),
        grid=(bsz, seq // tq),
        in_specs=[pl.BlockSpec((1, tq, A_Q_W), lambda b, i: (b, i, 0)),
                  pl.BlockSpec((1, seq, A_KV_W), lambda b, i: (b, 0, 0)),
                  pl.BlockSpec((1, seq, 2 * A_KV_W), lambda b, i: (b, 0, 0))],
        out_specs=pl.BlockSpec((1, tq, A_Q_W), lambda b, i: (b, i, 0)),
        out_shape=jax.ShapeDtypeStruct((bsz, seq, A_Q_W), BF16),
        scratch_shapes=[pltpu.VMEM((rows_all, A_KV_W), BF16),
                        pltpu.VMEM((2, rows_all, tk), BF16),
                        pltpu.VMEM((2, rows_all, tk), BF16),
                        pltpu.VMEM((rows_all, LANES), F32),
                        pltpu.VMEM((2, rows_all, LANES), F32),
                        pltpu.VMEM((rows_all, 2 * A_KV_W), F32)],
        compiler_params=pltpu.CompilerParams(
            dimension_semantics=("parallel", "parallel"), vmem_limit_bytes=VMEM_LIMIT),
        name="attn_a",
    )(qa, ka, va)


HALO = 128
SUB = 128
N_SIDE = 64
WIN = SUB + 2 * N_SIDE


def _attn_b_body(q_ref, kp_ref, kc_ref, kn_ref, vp_ref, vc_ref, vn_ref, bias_ref, o_ref, lse_ref,
                 kbuf, vbuf, *, seq_d):
    tu = q_ref.shape[2]
    i = pl.program_id(2)
    kbuf[0:HALO] = kp_ref[0, 0]
    kbuf[HALO:HALO + tu] = kc_ref[0, 0]
    kbuf[HALO + tu:] = kn_ref[0, 0]
    vbuf[0:HALO] = vp_ref[0, 0]
    vbuf[HALO:HALO + tu] = vc_ref[0, 0]
    vbuf[HALO + tu:] = vn_ref[0, 0]

    nh = B_HEADS_PER_GROUP
    nsub = tu // SUB
    head_of_lane = lax.broadcasted_iota(jnp.int32, (1, B_OUT_W), 1) // HEAD_DIM
    col = lax.broadcasted_iota(jnp.int32, (1, WIN), 1)

    def scores(j):
        qs = q_ref[0, 0, j * SUB:(j + 1) * SUB, :]
        qst = jnp.concatenate([jnp.where(head_of_lane == hh, qs, jnp.zeros_like(qs)) for hh in range(nh)],
                              axis=0)
        w0 = HALO - N_SIDE + j * SUB
        return _nt_dot(qst, kbuf[w0:w0 + WIN, :])

    s_next = scores(0)
    for j in range(nsub):
        s = s_next + bias_ref[...]
        if j + 1 < nsub:
            s_next = scores(j + 1)
        if j == 0 or j == nsub - 1:
            key_u = i * tu + j * SUB - N_SIDE + col
            s = jnp.where((key_u >= 0) & (key_u < seq_d), s, NEG_INF)
        m = jnp.max(s, axis=1, keepdims=True)
        p = jnp.exp(s - m)
        l = jnp.sum(p, axis=1, keepdims=True)
        w0 = HALO - N_SIDE + j * SUB
        pv = _dot(p.astype(BF16), vbuf[w0:w0 + WIN, :])
        on = pv / l
        lse = m + jnp.log(l)
        o_acc = jnp.zeros((SUB, B_OUT_W), F32)
        lse_acc = jnp.zeros((SUB, B_OUT_W), F32)
        for hh in range(nh):
            rows = slice(hh * SUB, (hh + 1) * SUB)
            mine = head_of_lane == hh
            o_acc = jnp.where(mine, on[rows], o_acc)
            lse_acc = jnp.where(mine, lse[rows], lse_acc)
        o_ref[0, 0, j * SUB:(j + 1) * SUB, :] = o_acc
        lse_ref[0, 0, j * SUB:(j + 1) * SUB, :] = lse_acc


def _band_bias(group, dil):
    row = np.arange(SUB)[:, None]
    col = np.arange(WIN)[None, :]
    rel = col - N_SIDE - row
    tiles = []
    for hh in range(B_HEADS_PER_GROUP):
        slope = 2.0 ** (-8.0 * (group * B_HEADS_PER_GROUP + hh + 1) / B_HEADS)
        tiles.append(np.where(np.abs(rel) <= N_SIDE, -slope * np.abs(rel) * dil, NEG_INF))
    return jnp.asarray(np.concatenate(tiles, axis=0), F32)


def _attn_b_group(q, k, v, group, dil, tu):
    bsz, _, seq_d, _ = q.shape
    tu = min(tu, seq_d)
    per = tu // HALO
    last = seq_d // HALO - 1
    bias = _band_bias(group, dil)
    cur = pl.BlockSpec((1, 1, tu, B_OUT_W), lambda b, r, i: (b, r, i, 0))
    prev = pl.BlockSpec((1, 1, HALO, B_OUT_W), lambda b, r, i: (b, r, jnp.maximum(i * per - 1, 0), 0))
    nxt = pl.BlockSpec((1, 1, HALO, B_OUT_W), lambda b, r, i: (b, r, jnp.minimum((i + 1) * per, last), 0))
    return pl.pallas_call(
        functools.partial(_attn_b_body, seq_d=seq_d),
        grid=(bsz, dil, seq_d // tu),
        in_specs=[cur, prev, cur, nxt, prev, cur, nxt,
                  pl.BlockSpec(bias.shape, lambda b, r, i: (0, 0))],
        out_specs=[cur, cur],
        out_shape=[jax.ShapeDtypeStruct((bsz, dil, seq_d, B_OUT_W), F32)] * 2,
        scratch_shapes=[pltpu.VMEM((tu + 2 * HALO, B_OUT_W), BF16)] * 2,
        compiler_params=pltpu.CompilerParams(
            dimension_semantics=("parallel", "parallel", "parallel"), vmem_limit_bytes=VMEM_LIMIT),
        name=f"attn_b{group}",
    )(q, k, k, k, v, v, v, bias)


def _merge_body(x_ref, oa_ref, o0_ref, o1_ref, o2_ref, l0_ref, l1_ref, l2_ref, ga_ref, gb_ref,
                woa_ref, wob_ref, wout_ref, g2_ref, wrh_ref, wrl_ref,
                x2_ref, h2_ref, aff_ref, o_scr, l_scr):
    tm = x_ref.shape[1]
    slabs = B_OUT_W // LANES
    for gi, (o_ref, lse_ref) in enumerate(((o0_ref, l0_ref), (o1_ref, l1_ref), (o2_ref, l2_ref))):
        dil = B_GROUPS[gi][1]
        for r in range(dil):
            for c in range(slabs):
                lanes = slice(c * LANES, (c + 1) * LANES)
                o_scr[gi * slabs + c, pl.ds(r, tm // dil, stride=dil), :] = o_ref[0, r, :, lanes]
                l_scr[gi * slabs + c, pl.ds(r, tm // dil, stride=dil), :] = lse_ref[0, r, :, lanes]
    whole = lambda scr, gi: jnp.concatenate([scr[gi * slabs + c] for c in range(slabs)], axis=1)
    l0, l1, l2 = whole(l_scr, 0), whole(l_scr, 1), whole(l_scr, 2)
    mx = jnp.maximum(jnp.maximum(l0, l1), l2)
    e0, e1, e2 = jnp.exp(l0 - mx), jnp.exp(l1 - mx), jnp.exp(l2 - mx)
    ob = (e0 * whole(o_scr, 0) + e1 * whole(o_scr, 1) + e2 * whole(o_scr, 2)) / (e0 + e1 + e2)
    ya = _dot(oa_ref[0], woa_ref[...])
    yb = _dot(ob.astype(BF16), wob_ref[...])
    mrg = ga_ref[0].astype(F32) * ya + gb_ref[0].astype(F32) * yb
    x2 = x_ref[0] + _dot(mrg.astype(BF16), wout_ref[...])
    x2_ref[0] = x2
    ms = jnp.mean(x2 * x2, axis=-1, keepdims=True)
    h2 = x2 * lax.rsqrt(ms + NORM_EPS) * g2_ref[...]
    h2_ref[0] = h2.astype(BF16)
    hh, hl = _split_bf16(h2)
    wh, wl = wrh_ref[...], wrl_ref[...]
    logits = _nt_dot(wh, hh) + _nt_dot(wh, hl) + _nt_dot(wl, hh)
    mxl = jnp.max(logits, axis=0, keepdims=True)
    ex = jnp.exp(logits - mxl)
    aff_ref[0] = ex / jnp.sum(ex, axis=0, keepdims=True)


def _merge(x, oa, obs, lses, gate_a, gate_b, w_o_a, w_o_b, w_out, norm2_g, w_router, tm):
    bsz, seq, d_model = x.shape
    row = lambda w: pl.BlockSpec((1, tm, w), lambda b, i: (b, i, 0))
    full = lambda a: pl.BlockSpec(a.shape, lambda b, i: (0,) * a.ndim)
    res = [pl.BlockSpec((1, dil, tm // dil, B_OUT_W), lambda b, i: (b, 0, i, 0)) for _, dil in B_GROUPS]
    woa, wob, wout = w_o_a.astype(BF16), w_o_b.astype(BF16), w_out.astype(BF16)
    g2 = norm2_g[None, :]
    wrh, wrl = _split_bf16(w_router.T)
    return pl.pallas_call(
        _merge_body,
        grid=(bsz, seq // tm),
        in_specs=[row(d_model), row(A_Q_W)] + res + res + [row(d_model)] * 2
                 + [full(woa), full(wob), full(wout), full(g2), full(wrh), full(wrl)],
        out_specs=[row(d_model), row(d_model),
                   pl.BlockSpec((1, N_EXPERTS, tm), lambda b, i: (b, 0, i))],
        out_shape=[jax.ShapeDtypeStruct((bsz, seq, d_model), F32),
                   jax.ShapeDtypeStruct((bsz, seq, d_model), BF16),
                   jax.ShapeDtypeStruct((bsz, N_EXPERTS, seq), F32)],
        scratch_shapes=[pltpu.VMEM((len(B_GROUPS) * B_OUT_W // LANES, tm, LANES), F32)] * 2,
        compiler_params=pltpu.CompilerParams(
            dimension_semantics=("parallel", "parallel"), vmem_limit_bytes=VMEM_LIMIT),
        name="merge",
    )(x, oa, *obs, *lses, gate_a, gate_b, woa, wob, wout, g2, wrh, wrl)


TOK_BLK = 256
GATHER_ROWS = 128
META_LO = 64
META_HI = 96


def _route_body(aff_ref, pos_ref, post_ref, gt_ref, meta_ref, *, cap):
    aff = aff_ref[0]
    n_e, seq = aff.shape
    bits = pltpu.bitcast(aff, jnp.int32)

    def refine(it, thr):
        cand = thr | jnp.left_shift(jnp.int32(1), 30 - it)
        cnt = jnp.sum(jnp.where(bits >= cand, 1.0, 0.0), axis=1, keepdims=True)
        return jnp.where(cnt >= cap, cand, thr)

    thr = lax.fori_loop(0, 31, refine, jnp.zeros((n_e, 1), jnp.int32))
    above = bits > thr
    tied = bits == thr
    need = cap - jnp.sum(jnp.where(above, 1.0, 0.0), axis=1, keepdims=True)

    ri = lax.broadcasted_iota(jnp.int32, (TOK_BLK, TOK_BLK), 0)
    ci = lax.broadcasted_iota(jnp.int32, (TOK_BLK, TOK_BLK), 1)
    before = jnp.where(ri < ci, 1.0, 0.0).astype(BF16)
    lane = lax.broadcasted_iota(jnp.int32, (1, LANES), 1)
    pad_rows = LANES - n_e

    run_tied = jnp.zeros((n_e, 1), F32)
    run_sel = jnp.zeros((n_e, 1), F32)
    starts = jnp.zeros((n_e, LANES), F32)
    ends = jnp.zeros((n_e, LANES), F32)
    n_blk = seq // TOK_BLK
    for c in range(n_blk):
        sl = slice(c * TOK_BLK, (c + 1) * TOK_BLK)
        tied_c = jnp.where(tied[:, sl], 1.0, 0.0)
        rank = run_tied + _dot(tied_c.astype(BF16), before)
        sel = above[:, sl] | (tied[:, sl] & (rank < need))
        sel_c = jnp.where(sel, 1.0, 0.0)
        pos = run_sel + _dot(sel_c.astype(BF16), before)
        pos_m = jnp.where(sel, pos, -1.0)
        g_m = jnp.where(sel, aff[:, sl], 0.0)
        pos_ref[0, :, sl] = pos_m
        fill = jnp.full((pad_rows, TOK_BLK), -1.0, F32)
        post_ref[0, sl, :] = jnp.concatenate([pos_m, fill], axis=0).T
        gt_ref[0, sl, :] = jnp.concatenate([g_m, jnp.zeros((pad_rows, TOK_BLK), F32)], axis=0).T
        starts = jnp.where(lane == c, run_sel, starts)
        run_tied = run_tied + jnp.sum(tied_c, axis=1, keepdims=True)
        run_sel = run_sel + jnp.sum(sel_c, axis=1, keepdims=True)
        ends = jnp.where(lane == c, run_sel, ends)
    meta = jnp.where(lane == n_blk, run_sel, starts)
    real = lane < n_blk
    for j in range(cap // GATHER_ROWS):
        c0 = float(j * GATHER_ROWS)
        first = jnp.sum(jnp.where(real & (ends <= c0), 1.0, 0.0), axis=1, keepdims=True)
        stop = n_blk - jnp.sum(jnp.where(real & (starts >= c0 + GATHER_ROWS), 1.0, 0.0), axis=1, keepdims=True)
        meta = jnp.where(lane == META_LO + j, first, meta)
        meta = jnp.where(lane == META_HI + j, stop, meta)
    meta_ref[0] = meta.astype(jnp.int32)


def _route(aff, cap):
    bsz, n_e, seq = aff.shape
    return pl.pallas_call(
        functools.partial(_route_body, cap=cap),
        grid=(bsz,),
        in_specs=[pl.BlockSpec((1, n_e, seq), lambda b: (b, 0, 0))],
        out_specs=[pl.BlockSpec((1, n_e, seq), lambda b: (b, 0, 0)),
                   pl.BlockSpec((1, seq, LANES), lambda b: (b, 0, 0)),
                   pl.BlockSpec((1, seq, LANES), lambda b: (b, 0, 0)),
                   pl.BlockSpec((1, n_e, LANES), lambda b: (b, 0, 0))],
        out_shape=[jax.ShapeDtypeStruct((bsz, n_e, seq), F32),
                   jax.ShapeDtypeStruct((bsz, seq, LANES), F32),
                   jax.ShapeDtypeStruct((bsz, seq, LANES), F32),
                   jax.ShapeDtypeStruct((bsz, n_e, LANES), jnp.int32)],
        compiler_params=pltpu.CompilerParams(
            dimension_semantics=("parallel",), vmem_limit_bytes=VMEM_LIMIT),
        name="route",
    )(aff)


GATHER_SPAN = 6


def _gather_body(meta_ref, h_ref, pos_ref, xin_ref, *, cap):
    b, e = pl.program_id(0), pl.program_id(1)
    base = (b * N_EXPERTS + e) * LANES
    n_blk = pos_ref.shape[2]
    half = GATHER_SPAN // 2
    slot = lax.broadcasted_iota(jnp.int32, (GATHER_ROWS, 1), 0)

    def picked(want, blk0, begin):
        pieces = []
        for t in range(half):
            pos = pos_ref[0, 0, pl.ds(blk0 + t, 1), :]
            if begin is not None:
                pos = jnp.where(blk0 + t >= begin, pos, -1.0)
            pieces.append(jnp.where(pos == want, 1.0, 0.0))
        onehot = jnp.concatenate(pieces, axis=1).astype(BF16)
        hblk = h_ref[0, pl.ds(pl.multiple_of(blk0 * TOK_BLK, TOK_BLK), half * TOK_BLK), :]
        return _dot(onehot, hblk)

    tiles = range(cap // GATHER_ROWS)
    for j in tiles:
        rows = slice(j * GATHER_ROWS, (j + 1) * GATHER_ROWS)
        want = (slot + j * GATHER_ROWS).astype(F32)
        blk0 = jnp.minimum(meta_ref[base + META_LO + j], n_blk - GATHER_SPAN)
        xin_ref[0, 0, rows, :] = (picked(want, blk0, None) + picked(want, blk0 + half, None)).astype(xin_ref.dtype)
    for j in tiles:
        rows = slice(j * GATHER_ROWS, (j + 1) * GATHER_ROWS)
        first = meta_ref[base + META_LO + j]
        stop = meta_ref[base + META_HI + j]

        @pl.when(stop > first + GATHER_SPAN)
        def _(j=j, rows=rows, first=first, stop=stop):
            want = (slot + j * GATHER_ROWS).astype(F32)

            def more(i, carry):
                begin = first + GATHER_SPAN + half * i
                extra = picked(want, jnp.minimum(begin, n_blk - half), begin)
                xin_ref[0, 0, rows, :] = (xin_ref[0, 0, rows, :].astype(F32) + extra).astype(xin_ref.dtype)
                return carry

            lax.fori_loop(0, (stop - first - GATHER_SPAN + half - 1) // half, more, 0)


def _gather(meta_flat, h2, pos, cap):
    bsz, seq, d_model = h2.shape
    n_e = pos.shape[1]
    assert seq // TOK_BLK >= GATHER_SPAN
    pos4 = pos.reshape(bsz, n_e, seq // TOK_BLK, TOK_BLK)
    return pl.pallas_call(
        functools.partial(_gather_body, cap=cap),
        grid_spec=pltpu.PrefetchScalarGridSpec(
            num_scalar_prefetch=1,
            grid=(bsz, n_e),
            in_specs=[pl.BlockSpec((1, seq, d_model), lambda b, e, m: (b, 0, 0)),
                      pl.BlockSpec((1, 1, seq // TOK_BLK, TOK_BLK), lambda b, e, m: (b, e, 0, 0))],
            out_specs=pl.BlockSpec((1, 1, cap, d_model), lambda b, e, m: (b, e, 0, 0))),
        out_shape=jax.ShapeDtypeStruct((bsz, n_e, cap, d_model), BF16),
        compiler_params=pltpu.CompilerParams(
            dimension_semantics=("parallel", "parallel"), vmem_limit_bytes=VMEM_LIMIT),
        name="gather",
    )(meta_flat, h2, pos4)


FFN_ROWS = 256


def _ffn_body(x_ref, wg_ref, wu_ref, wd_ref, y_ref):
    cap = x_ref.shape[2]
    for r in range(cap // FFN_ROWS):
        rows = slice(r * FFN_ROWS, (r + 1) * FFN_ROWS)
        xin = x_ref[0, 0, rows, :]
        a = _dot(xin, wg_ref[0])
        u = _dot(xin, wu_ref[0])
        mid = (a * jax.nn.sigmoid(a) * u).astype(BF16)
        y_ref[0, 0, rows, :] = _dot(mid, wd_ref[0]).astype(y_ref.dtype)


def _ffn(xin, w_gate, w_up, w_down):
    bsz, n_e, cap, d_model = xin.shape
    ff = w_gate.shape[2]
    tok = pl.BlockSpec((1, 1, cap, d_model), lambda e, b: (b, e, 0, 0))
    return pl.pallas_call(
        _ffn_body,
        grid=(n_e, bsz),
        in_specs=[tok,
                  pl.BlockSpec((1, d_model, ff), lambda e, b: (e, 0, 0)),
                  pl.BlockSpec((1, d_model, ff), lambda e, b: (e, 0, 0)),
                  pl.BlockSpec((1, ff, d_model), lambda e, b: (e, 0, 0))],
        out_specs=tok,
        out_shape=jax.ShapeDtypeStruct((bsz, n_e, cap, d_model), BF16),
        compiler_params=pltpu.CompilerParams(
            dimension_semantics=("parallel", "parallel"), vmem_limit_bytes=VMEM_LIMIT),
        name="ffn",
    )(xin, w_gate.astype(BF16), w_up.astype(BF16), w_down.astype(BF16))


SCATTER_WIN = 128
SCATTER_ALIGN = 64
SCATTER_EXPERTS = 4


def _scatter_body(meta_ref, y_ref, post_ref, gt_ref, x2_ref, gf_ref, o_ref, *, tc):
    b, t, eg = pl.program_id(0), pl.program_id(1), pl.program_id(2)
    cap = y_ref.shape[2]
    n_blk = tc // TOK_BLK

    @pl.when(eg == 0)
    def _():
        o_ref[0] = x2_ref[0]

    lane = lax.broadcasted_iota(jnp.int32, (1, LANES), 1)
    slot = lax.broadcasted_iota(jnp.int32, (1, SCATTER_WIN), 1)

    def expert_range(kk, el):
        base = (b * N_EXPERTS + eg * SCATTER_EXPERTS + el) * LANES + t * n_blk + kk
        first = meta_ref[base]
        last = meta_ref[base + 1]
        start = jnp.minimum((first // SCATTER_ALIGN) * SCATTER_ALIGN, cap - SCATTER_WIN)
        return start, last

    def columns(kk, el):
        rows = slice(kk * TOK_BLK, (kk + 1) * TOK_BLK)
        mine = lane == eg * SCATTER_EXPERTS + el
        pcol = jnp.sum(jnp.where(mine, post_ref[0, rows, :], 0.0), axis=1, keepdims=True)
        gcol = jnp.sum(jnp.where(mine, gt_ref[0, rows, :], 0.0), axis=1, keepdims=True)
        return pcol, gcol

    def window(el, start):
        return y_ref[0, el, pl.ds(pl.multiple_of(start, SCATTER_ALIGN), SCATTER_WIN), :]

    for kk in range(n_blk):
        rows = slice(kk * TOK_BLK, (kk + 1) * TOK_BLK)
        for pair in range(SCATTER_EXPERTS // 2):
            sel, ywin = [], []
            for el in (2 * pair, 2 * pair + 1):
                start, _ = expert_range(kk, el)
                pcol, gcol = columns(kk, el)
                sel.append(jnp.where(pcol == (slot + start).astype(F32), gcol, 0.0))
                ywin.append(window(el, start))
            o_ref[0, rows, :] += _dot(jnp.concatenate(sel, axis=1).astype(BF16), jnp.concatenate(ywin, axis=0))

    for kk in range(n_blk):
        rows = slice(kk * TOK_BLK, (kk + 1) * TOK_BLK)
        for el in range(SCATTER_EXPERTS):
            start, last = expert_range(kk, el)

            @pl.when(last > start + SCATTER_WIN)
            def _(kk=kk, el=el, rows=rows, start=start, last=last):
                pcol, gcol = columns(kk, el)

                def more(j, carry):
                    begin = start + (j + 1) * SCATTER_WIN
                    begin_c = jnp.minimum(begin, cap - SCATTER_WIN)
                    want = slot + begin_c
                    sel = jnp.where((pcol == want.astype(F32)) & (want >= begin), gcol, 0.0)
                    o_ref[0, rows, :] += _dot(sel.astype(BF16), window(el, begin_c))
                    return carry

                lax.fori_loop(0, (last - start - 1) // SCATTER_WIN, more, 0)

    @pl.when(eg == pl.num_programs(2) - 1)
    def _():
        v = o_ref[0]
        ms = jnp.mean(v * v, axis=-1, keepdims=True)
        o_ref[0] = v * lax.rsqrt(ms + NORM_EPS) * gf_ref[...]


def _scatter(meta_flat, y, post, gt, x2, normf_g, tc):
    bsz, n_e, cap, d_model = y.shape
    seq = x2.shape[1]
    tc = min(tc, seq)
    gf = normf_g[None, :]
    assert n_e % SCATTER_EXPERTS == 0 and cap % SCATTER_ALIGN == 0 and cap >= SCATTER_WIN
    tokw = lambda w, **kw: pl.BlockSpec((1, tc, w), lambda b, t, e, m: (b, t, 0), **kw)
    return pl.pallas_call(
        functools.partial(_scatter_body, tc=tc),
        grid_spec=pltpu.PrefetchScalarGridSpec(
            num_scalar_prefetch=1,
            grid=(bsz, seq // tc, n_e // SCATTER_EXPERTS),
            in_specs=[pl.BlockSpec((1, SCATTER_EXPERTS, cap, d_model), lambda b, t, e, m: (b, e, 0, 0)),
                      tokw(LANES), tokw(LANES),
                      tokw(d_model, pipeline_mode=pl.Buffered(1)),
                      pl.BlockSpec(gf.shape, lambda b, t, e, m: (0, 0))],
            out_specs=tokw(d_model)),
        out_shape=jax.ShapeDtypeStruct((bsz, seq, d_model), F32),
        compiler_params=pltpu.CompilerParams(
            dimension_semantics=("parallel", "parallel", "arbitrary"), vmem_limit_bytes=VMEM_LIMIT),
        name="scatter",
    )(meta_flat, y, post, gt, x2, gf)


def kernel(x, norm1_g, w_in, b_gates, q_norm_g, k_norm_g, w_o_a, w_o_b, w_out, norm2_g, w_router,
           w_gate, w_up, w_down, normf_g):
    bsz, seq, d_model = x.shape
    depth = norm1_g.shape[0]
    cap = EC_CAPACITY_FACTOR * seq // N_EXPERTS
    tm = min(512, seq)
    for l in range(depth):
        qa, ka, va, gate_a, gate_b, qkv_b = _in_proj(
            x, norm1_g[l], w_in[l], b_gates[l], q_norm_g[l], k_norm_g[l], tm)
        oa = _attn_a(qa, ka, va, tq=min(128, seq), tk=min(1024, seq))
        obs, lses = [], []
        for gi, (_, dil) in enumerate(B_GROUPS):
            o, lse = _attn_b_group(*qkv_b[gi], gi, dil, tu=512)
            obs.append(o)
            lses.append(lse)
        x2, h2, aff = _merge(x, oa, obs, lses, gate_a, gate_b, w_o_a[l], w_o_b[l], w_out[l],
                             norm2_g[l], w_router[l], tm)
        pos, post, gt, meta = _route(aff, cap)
        meta_flat = meta.reshape(-1)
        xin = _gather(meta_flat, h2, pos, cap)
        y = _ffn(xin, w_gate[l], w_up[l], w_down[l])
        assert depth == 1
        x = _scatter(meta_flat, y, post, gt, x2, normf_g, tc=2048)
    return x
```

```python
import functools
from typing import NamedTuple

import numpy as np
import jax
import jax.numpy as jnp
from jax import lax
from jax.experimental import pallas as pl
from jax.experimental.pallas import tpu as pltpu

HEAD_DIM = 64
A_Q_HEADS = 8
A_KV_HEADS = 2
A_GROUP = A_Q_HEADS // A_KV_HEADS
B_GROUPS = ((128, 1), (512, 4), (2048, 16))
B_HEADS_PER_GROUP = 4
B_HEADS = B_HEADS_PER_GROUP * len(B_GROUPS)
GRID_W = 64
ROPE_THETA = 10000.0
N_EXPERTS = 16
EC_CAPACITY_FACTOR = 2
NORM_EPS = 1e-6
NEG_INF = -1e30

A_Q_W = A_Q_HEADS * HEAD_DIM
A_KV_W = A_KV_HEADS * HEAD_DIM
B_W = B_HEADS * HEAD_DIM
B_OUT_W = B_HEADS_PER_GROUP * HEAD_DIM

LANES = 128
MXU_DIM = 256
VMEM_LIMIT = 56 * 1024 * 1024

F32 = jnp.float32
BF16 = jnp.bfloat16
LOG2_E = 1.4426950408889634


def _nt_dot(a, b):
    return lax.dot_general(a, b, (((1,), (1,)), ((), ())), preferred_element_type=F32)


def _dot(a, b):
    return jnp.dot(a, b, preferred_element_type=F32)


def _split_bf16(v):
    hi = v.astype(BF16)
    lo = (v - hi.astype(F32)).astype(BF16)
    return hi, lo


def _in_proj_body(x_ref, g1_ref, w_ref, bg_ref, qg_ref, kg_ref, cos_ref, sin_ref, hs_ref,
                  qa_ref, ka_ref, va_ref, ga_ref, gb_ref, *rest, d_model):
    b_refs, zb_scr = rest[:-1], rest[-1]
    tm = x_ref.shape[1]
    x = x_ref[0]
    ms = jnp.mean(x * x, axis=-1, keepdims=True)
    h = (x * lax.rsqrt(ms + NORM_EPS) * g1_ref[...]).astype(BF16)

    def proj(lo, width):
        return _dot(h, w_ref[:, lo:lo + width])

    cos = cos_ref[...]
    sin = sin_ref[...]
    lane = lax.broadcasted_iota(jnp.int32, (1, LANES), 1)
    low = (lane % 32) < 16

    def head_norm_rope(z, gain, nrep):
        width = LANES * nrep
        ss = _dot((z * z).astype(BF16), hs_ref[:width, :width])
        zn = z * lax.rsqrt(ss * (1.0 / HEAD_DIM) + NORM_EPS) * gain
        rep = lambda t: jnp.concatenate([t] * nrep, axis=1) if nrep > 1 else t
        partner = jnp.where(rep(low), pltpu.roll(zn, width - 16, 1), pltpu.roll(zn, 16, 1))
        return zn * rep(cos) + partner * rep(sin)

    scale = HEAD_DIM ** -0.5
    za = proj(0, A_Q_W + 2 * A_KV_W)
    qa = head_norm_rope(za[:, :A_Q_W], qg_ref[...], A_Q_W // LANES) * (scale * LOG2_E)
    ka = head_norm_rope(za[:, A_Q_W:A_Q_W + A_KV_W], kg_ref[...], A_KV_W // LANES)
    qa_ref[0] = qa.astype(BF16)
    ka_ref[0] = ka.astype(BF16)
    va = za[:, A_Q_W + A_KV_W:]
    va_ref[0] = jnp.concatenate([va, jnp.ones_like(va)], axis=1).astype(BF16)
    off = A_Q_W + 2 * A_KV_W
    slabs = B_OUT_W // LANES
    for t in range(3):
        z = proj(off + t * B_W, B_W) * (scale if t == 0 else 1.0)
        for c in range(B_W // LANES):
            zb_scr[c] = z[:, c * LANES:(c + 1) * LANES]
        for gi, (_, dil) in enumerate(B_GROUPS):
            out = b_refs[3 * gi + t]
            for r in range(dil):
                for c in range(slabs):
                    out[0, r, :, c * LANES:(c + 1) * LANES] = (
                        zb_scr[gi * slabs + c, pl.ds(r, tm // dil, stride=dil), :].astype(BF16))
    off += 3 * B_W
    ga_ref[0] = jax.nn.sigmoid(proj(off, d_model) + bg_ref[:, :d_model]).astype(BF16)
    gb_ref[0] = jax.nn.sigmoid(proj(off + d_model, d_model) + bg_ref[:, d_model:]).astype(BF16)


def _rope_tables(seq):
    rows = seq // GRID_W
    row_id = jnp.repeat(jnp.arange(rows, dtype=F32), GRID_W)
    col_id = jnp.tile(jnp.arange(GRID_W, dtype=F32), rows)
    half = HEAD_DIM // 2
    inv_freq = 1.0 / (ROPE_THETA ** (jnp.arange(0, half, 2, dtype=F32) / half))
    ang_r = row_id[:, None] * inv_freq[None, :]
    ang_c = col_id[:, None] * inv_freq[None, :]
    cos = jnp.concatenate([jnp.cos(ang_r)] * 2 + [jnp.cos(ang_c)] * 2, axis=1)
    sin = jnp.concatenate([-jnp.sin(ang_r), jnp.sin(ang_r), -jnp.sin(ang_c), jnp.sin(ang_c)], axis=1)
    return jnp.tile(cos, (1, 2)), jnp.tile(sin, (1, 2))


def _in_proj(x, norm1_g, w_in, b_gates, q_norm_g, k_norm_g, tm):
    bsz, seq, d_model = x.shape
    cos, sin = _rope_tables(seq)
    head_id = np.arange(A_Q_W) // HEAD_DIM
    hs = jnp.asarray(head_id[:, None] == head_id[None, :], BF16)
    qg = jnp.tile(q_norm_g, A_Q_W // HEAD_DIM)[None, :]
    kg = jnp.tile(k_norm_g, A_KV_W // HEAD_DIM)[None, :]
    row = lambda w: pl.BlockSpec((1, tm, w), lambda b, i: (b, i, 0))
    full = lambda a: pl.BlockSpec(a.shape, lambda b, i: (0,) * a.ndim)
    tab = pl.BlockSpec((tm, LANES), lambda b, i: (i, 0))
    g1 = norm1_g[None, :]
    bg = b_gates[None, :]
    w = w_in.astype(BF16)
    shp = lambda wd: jax.ShapeDtypeStruct((bsz, seq, wd), BF16)
    b_specs, b_shapes = [], []
    for _, dil in B_GROUPS:
        assert tm % (16 * dil) == 0
        b_specs += [pl.BlockSpec((1, dil, tm // dil, B_OUT_W), lambda b, i: (b, 0, i, 0))] * 3
        b_shapes += [jax.ShapeDtypeStruct((bsz, dil, seq // dil, B_OUT_W), BF16)] * 3
    outs = pl.pallas_call(
        functools.partial(_in_proj_body, d_model=d_model),
        grid=(bsz, seq // tm),
        in_specs=[row(d_model), full(g1), full(w), full(bg), full(qg), full(kg), tab, tab, full(hs)],
        out_specs=[row(A_Q_W), row(A_KV_W), row(2 * A_KV_W), row(d_model), row(d_model)] + b_specs,
        out_shape=[shp(A_Q_W), shp(A_KV_W), shp(2 * A_KV_W), shp(d_model), shp(d_model)] + b_shapes,
        scratch_shapes=[pltpu.VMEM((B_W // LANES, tm, LANES), F32)],
        compiler_params=pltpu.CompilerParams(
            dimension_semantics=("parallel", "parallel"), vmem_limit_bytes=VMEM_LIMIT),
        name="in_proj",
    )(x, g1, w, bg, qg, kg, cos, sin, hs)
    qa, ka, va, gate_a, gate_b = outs[:5]
    qkv_b = [outs[5 + 3 * gi:8 + 3 * gi] for gi in range(len(B_GROUPS))]
    return qa, ka, va, gate_a, gate_b, qkv_b


def _attn_a_body(q_ref, k_ref, v_ref, o_ref, qp_scr, s_scr, p_scr, m_scr, a_scr, acc_scr, *, tk):
    tq = q_ref.shape[1]
    seq = k_ref.shape[1]
    nk = seq // tk
    rows_all = A_Q_HEADS * tq
    q = q_ref[0].astype(F32)
    zeros = jnp.zeros((tq, HEAD_DIM), F32)
    for g in range(A_Q_HEADS):
        qg = q[:, g * HEAD_DIM:(g + 1) * HEAD_DIM]
        qp = jnp.concatenate([qg, zeros] if g // A_GROUP == 0 else [zeros, qg], axis=1)
        qp_scr[g * tq:(g + 1) * tq, :] = qp.astype(BF16)
    m_scr[...] = jnp.full_like(m_scr, NEG_INF)
    acc_scr[...] = jnp.zeros_like(acc_scr)

    def scores(kt, slot):
        start = pl.multiple_of(kt * tk, tk)
        s_scr[slot] = _nt_dot(qp_scr[...], k_ref[0, pl.ds(start, tk), :]).astype(BF16)

    def softmax_pv(kt, slot, next_kt=None, next_slot=None):
        if next_kt is not None:
            scores(next_kt, next_slot)
        for c in range(rows_all // ATTN_A_CHUNK):
            rows = slice(c * ATTN_A_CHUNK, (c + 1) * ATTN_A_CHUNK)
            s = s_scr[slot, rows, :]
            blocks = [s[:, j * LANES:(j + 1) * LANES] for j in range(tk // LANES)]
            while len(blocks) > 1:
                blocks = [jnp.maximum(a, b) for a, b in zip(blocks[::2], blocks[1::2])]
            m_prev = m_scr[rows, :]
            m_new = jnp.maximum(m_prev, jnp.max(blocks[0].astype(F32), axis=1, keepdims=True))
            m_scr[rows, :] = m_new
            a_scr[slot, rows, :] = jnp.exp2(m_prev - m_new)
            p_scr[slot, rows, :] = jnp.exp2(s - jnp.concatenate([m_new.astype(BF16)] * (tk // LANES), axis=1))
        start = pl.multiple_of(kt * tk, tk)
        pv = _dot(p_scr[slot], v_ref[0, pl.ds(start, tk), :])
        alpha = a_scr[slot]
        acc_scr[...] = jnp.concatenate([alpha, alpha], axis=1) * acc_scr[...] + pv

    scores(0, 0)
    for kt in range(nk):
        if kt + 1 < nk:
            softmax_pv(kt, kt % 2, kt + 1, (kt + 1) % 2)
        else:
            softmax_pv(kt, kt % 2)

    outs = []
    for g in range(A_Q_HEADS):
        kv = g // A_GROUP
        rows = slice(g * tq, (g + 1) * tq)
        outs.append(acc_scr[rows, kv * HEAD_DIM:(kv + 1) * HEAD_DIM]
                    / acc_scr[rows, A_KV_W:A_KV_W + HEAD_DIM])
    o_ref[0] = jnp.concatenate(outs, axis=1).astype(o_ref.dtype)


ATTN_A_CHUNK = 128


def _attn_a(qa, ka, va, tq, tk):
    bsz, seq, _ = qa.shape
    rows_all = A_Q_HEADS * tq
    return pl.pallas_call(
        functools.partial(_attn_a_body, tk=tk),
        grid=(bsz, seq // tq),
        in_specs=[pl.BlockSpec((1, tq, A_Q_W), lambda b, i: (b, i, 0)),
                  pl.BlockSpec((1, seq, A_KV_W), lambda b, i: (b, 0, 0)),
                  pl.BlockSpec((1, seq, 2 * A_KV_W), lambda b, i: (b, 0, 0))],
        out_specs=pl.BlockSpec((1, tq, A_Q_W), lambda b, i: (b, i, 0)),
        out_shape=jax.ShapeDtypeStruct((bsz, seq, A_Q_W), BF16),
        scratch_shapes=[pltpu.VMEM((rows_all, A_KV_W), BF16),
                        pltpu.VMEM((2, rows_all, tk), BF16),
                        pltpu.VMEM((2, rows_all, tk), BF16),
                        pltpu.VMEM((rows_all, LANES), F32),
                        pltpu.VMEM((2, rows_all, LANES), F32),
                        pltpu.VMEM((rows_all, 2 * A_KV_W), F32)],
        compiler_params=pltpu.CompilerParams(
            dimension_semantics=("parallel", "parallel"), vmem_limit_bytes=VMEM_LIMIT),
        name="attn_a",
    )(qa, ka, va)


HALO = 128
SUB = 128
N_SIDE = 64
WIN = SUB + 2 * N_SIDE


def _attn_b_body(q_ref, kp_ref, kc_ref, kn_ref, vp_ref, vc_ref, vn_ref, bias_ref, o_ref, lse_ref,
                 kbuf, vbuf, *, seq_d):
    tu = q_ref.shape[2]
    i = pl.program_id(2)
    kbuf[0:HALO] = kp_ref[0, 0]
    kbuf[HALO:HALO + tu] = kc_ref[0, 0]
    kbuf[HALO + tu:] = kn_ref[0, 0]
    vbuf[0:HALO] = vp_ref[0, 0]
    vbuf[HALO:HALO + tu] = vc_ref[0, 0]
    vbuf[HALO + tu:] = vn_ref[0, 0]

    nh = B_HEADS_PER_GROUP
    nsub = tu // SUB
    head_of_lane = lax.broadcasted_iota(jnp.int32, (1, B_OUT_W), 1) // HEAD_DIM
    col = lax.broadcasted_iota(jnp.int32, (1, WIN), 1)

    def scores(j):
        qs = q_ref[0, 0, j * SUB:(j + 1) * SUB, :]
        qst = jnp.concatenate([jnp.where(head_of_lane == hh, qs, jnp.zeros_like(qs)) for hh in range(nh)],
                              axis=0)
        w0 = HALO - N_SIDE + j * SUB
        return _nt_dot(qst, kbuf[w0:w0 + WIN, :])

    s_next = scores(0)
    for j in range(nsub):
        s = s_next + bias_ref[...]
        if j + 1 < nsub:
            s_next = scores(j + 1)
        if j == 0 or j == nsub - 1:
            key_u = i * tu + j * SUB - N_SIDE + col
            s = jnp.where((key_u >= 0) & (key_u < seq_d), s, NEG_INF)
        m = jnp.max(s, axis=1, keepdims=True)
        p = jnp.exp(s - m)
        l = jnp.sum(p, axis=1, keepdims=True)
        w0 = HALO - N_SIDE + j * SUB
        pv = _dot(p.astype(BF16), vbuf[w0:w0 + WIN, :])
        on = pv / l
        lse = m + jnp.log(l)
        o_acc = jnp.zeros((SUB, B_OUT_W), F32)
        lse_acc = jnp.zeros((SUB, B_OUT_W), F32)
        for hh in range(nh):
            rows = slice(hh * SUB, (hh + 1) * SUB)
            mine = head_of_lane == hh
            o_acc = jnp.where(mine, on[rows], o_acc)
            lse_acc = jnp.where(mine, lse[rows], lse_acc)
        o_ref[0, 0, j * SUB:(j + 1) * SUB, :] = o_acc
        lse_ref[0, 0, j * SUB:(j + 1) * SUB, :] = lse_acc


def _band_bias(group, dil):
    row = np.arange(SUB)[:, None]
    col = np.arange(WIN)[None, :]
    rel = col - N_SIDE - row
    tiles = []
    for hh in range(B_HEADS_PER_GROUP):
        slope = 2.0 ** (-8.0 * (group * B_HEADS_PER_GROUP + hh + 1) / B_HEADS)
        tiles.append(np.where(np.abs(rel) <= N_SIDE, -slope * np.abs(rel) * dil, NEG_INF))
    return jnp.asarray(np.concatenate(tiles, axis=0), F32)


def _attn_b_group(q, k, v, group, dil, tu):
    bsz, _, seq_d, _ = q.shape
    tu = min(tu, seq_d)
    per = tu // HALO
    last = seq_d // HALO - 1
    bias = _band_bias(group, dil)
    cur = pl.BlockSpec((1, 1, tu, B_OUT_W), lambda b, r, i: (b, r, i, 0))
    prev = pl.BlockSpec((1, 1, HALO, B_OUT_W), lambda b, r, i: (b, r, jnp.maximum(i * per - 1, 0), 0))
    nxt = pl.BlockSpec((1, 1, HALO, B_OUT_W), lambda b, r, i: (b, r, jnp.minimum((i + 1) * per, last), 0))
    return pl.pallas_call(
        functools.partial(_attn_b_body, seq_d=seq_d),
        grid=(bsz, dil, seq_d // tu),
        in_specs=[cur, prev, cur, nxt, prev, cur, nxt,
                  pl.BlockSpec(bias.shape, lambda b, r, i: (0, 0))],
        out_specs=[cur, cur],
        out_shape=[jax.ShapeDtypeStruct((bsz, dil, seq_d, B_OUT_W), F32)] * 2,
        scratch_shapes=[pltpu.VMEM((tu + 2 * HALO, B_OUT_W), BF16)] * 2,
        compiler_params=pltpu.CompilerParams(
            dimension_semantics=("parallel", "parallel", "parallel"), vmem_limit_bytes=VMEM_LIMIT),
        name=f"attn_b{group}",
    )(q, k, k, k, v, v, v, bias)


def _merge_body(x_ref, oa_ref, o0_ref, o1_ref, o2_ref, l0_ref, l1_ref, l2_ref, ga_ref, gb_ref,
                woa_ref, wob_ref, wout_ref, g2_ref, wrh_ref, wrl_ref,
                x2_ref, h2_ref, aff_ref, o_scr, l_scr):
    tm = x_ref.shape[1]
    slabs = B_OUT_W // LANES
    for gi, (o_ref, lse_ref) in enumerate(((o0_ref, l0_ref), (o1_ref, l1_ref), (o2_ref, l2_ref))):
        dil = B_GROUPS[gi][1]
        for r in range(dil):
            for c in range(slabs):
                lanes = slice(c * LANES, (c + 1) * LANES)
                o_scr[gi * slabs + c, pl.ds(r, tm // dil, stride=dil), :] = o_ref[0, r, :, lanes]
                l_scr[gi * slabs + c, pl.ds(r, tm // dil, stride=dil), :] = lse_ref[0, r, :, lanes]
    whole = lambda scr, gi: jnp.concatenate([scr[gi * slabs + c] for c in range(slabs)], axis=1)
    l0, l1, l2 = whole(l_scr, 0), whole(l_scr, 1), whole(l_scr, 2)
    mx = jnp.maximum(jnp.maximum(l0, l1), l2)
    e0, e1, e2 = jnp.exp(l0 - mx), jnp.exp(l1 - mx), jnp.exp(l2 - mx)
    ob = (e0 * whole(o_scr, 0) + e1 * whole(o_scr, 1) + e2 * whole(o_scr, 2)) / (e0 + e1 + e2)
    ya = _dot(oa_ref[0], woa_ref[...])
    yb = _dot(ob.astype(BF16), wob_ref[...])
    mrg = ga_ref[0].astype(F32) * ya + gb_ref[0].astype(F32) * yb
    x2 = x_ref[0] + _dot(mrg.astype(BF16), wout_ref[...])
    x2_ref[0] = x2
    ms = jnp.mean(x2 * x2, axis=-1, keepdims=True)
    h2 = x2 * lax.rsqrt(ms + NORM_EPS) * g2_ref[...]
    h2_ref[0] = h2.astype(BF16)
    hh, hl = _split_bf16(h2)
    wh, wl = wrh_ref[...], wrl_ref[...]
    logits = _nt_dot(wh, hh) + _nt_dot(wh, hl) + _nt_dot(wl, hh)
    mxl = jnp.max(logits, axis=0, keepdims=True)
    ex = jnp.exp(logits - mxl)
    aff_ref[0] = ex / jnp.sum(ex, axis=0, keepdims=True)


def _merge(x, oa, obs, lses, gate_a, gate_b, w_o_a, w_o_b, w_out, norm2_g, w_router, tm):
    bsz, seq, d_model = x.shape
    row = lambda w: pl.BlockSpec((1, tm, w), lambda b, i: (b, i, 0))
    full = lambda a: pl.BlockSpec(a.shape, lambda b, i: (0,) * a.ndim)
    res = [pl.BlockSpec((1, dil, tm // dil, B_OUT_W), lambda b, i: (b, 0, i, 0)) for _, dil in B_GROUPS]
    woa, wob, wout = w_o_a.astype(BF16), w_o_b.astype(BF16), w_out.astype(BF16)
    g2 = norm2_g[None, :]
    wrh, wrl = _split_bf16(w_router.T)
    return pl.pallas_call(
        _merge_body,
        grid=(bsz, seq // tm),
        in_specs=[row(d_model), row(A_Q_W)] + res + res + [row(d_model)] * 2
                 + [full(woa), full(wob), full(wout), full(g2), full(wrh), full(wrl)],
        out_specs=[row(d_model), row(d_model),
                   pl.BlockSpec((1, N_EXPERTS, tm), lambda b, i: (b, 0, i))],
        out_shape=[jax.ShapeDtypeStruct((bsz, seq, d_model), F32),
                   jax.ShapeDtypeStruct((bsz, seq, d_model), BF16),
                   jax.ShapeDtypeStruct((bsz, N_EXPERTS, seq), F32)],
        scratch_shapes=[pltpu.VMEM((len(B_GROUPS) * B_OUT_W // LANES, tm, LANES), F32)] * 2,
        compiler_params=pltpu.CompilerParams(
            dimension_semantics=("parallel", "parallel"), vmem_limit_bytes=VMEM_LIMIT),
        name="merge",
    )(x, oa, *obs, *lses, gate_a, gate_b, woa, wob, wout, g2, wrh, wrl)


TOK_BLK = MXU_DIM
GATHER_ROWS = 128
META_LO = 64
META_HI = 96


def _route_body(aff_ref, pos_ref, post_ref, gt_ref, meta_ref, *, cap):
    aff = aff_ref[0]
    n_e, seq = aff.shape
    bits = pltpu.bitcast(aff, jnp.int32)

    def refine(it, thr):
        cand = thr | jnp.left_shift(jnp.int32(1), 30 - it)
        cnt = jnp.sum(jnp.where(bits >= cand, 1.0, 0.0), axis=1, keepdims=True)
        return jnp.where(cnt >= cap, cand, thr)

    thr = lax.fori_loop(0, 31, refine, jnp.zeros((n_e, 1), jnp.int32))
    above = bits > thr
    tied = bits == thr
    need = cap - jnp.sum(jnp.where(above, 1.0, 0.0), axis=1, keepdims=True)

    ri = lax.broadcasted_iota(jnp.int32, (TOK_BLK, TOK_BLK), 0)
    ci = lax.broadcasted_iota(jnp.int32, (TOK_BLK, TOK_BLK), 1)
    before = jnp.where(ri < ci, 1.0, 0.0).astype(BF16)
    lane = lax.broadcasted_iota(jnp.int32, (1, LANES), 1)
    pad_rows = LANES - n_e

    run_tied = jnp.zeros((n_e, 1), F32)
    run_sel = jnp.zeros((n_e, 1), F32)
    starts = jnp.zeros((n_e, LANES), F32)
    ends = jnp.zeros((n_e, LANES), F32)
    n_blk = seq // TOK_BLK
    for c in range(n_blk):
        sl = slice(c * TOK_BLK, (c + 1) * TOK_BLK)
        tied_c = jnp.where(tied[:, sl], 1.0, 0.0)
        rank = run_tied + _dot(tied_c.astype(BF16), before)
        sel = above[:, sl] | (tied[:, sl] & (rank < need))
        sel_c = jnp.where(sel, 1.0, 0.0)
        pos = run_sel + _dot(sel_c.astype(BF16), before)
        pos_m = jnp.where(sel, pos, -1.0)
        g_m = jnp.where(sel, aff[:, sl], 0.0)
        pos_ref[0, :, sl] = pos_m
        fill = jnp.full((pad_rows, TOK_BLK), -1.0, F32)
        post_ref[0, sl, :] = jnp.concatenate([pos_m, fill], axis=0).T
        gt_ref[0, sl, :] = jnp.concatenate([g_m, jnp.zeros((pad_rows, TOK_BLK), F32)], axis=0).T
        starts = jnp.where(lane == c, run_sel, starts)
        run_tied = run_tied + jnp.sum(tied_c, axis=1, keepdims=True)
        run_sel = run_sel + jnp.sum(sel_c, axis=1, keepdims=True)
        ends = jnp.where(lane == c, run_sel, ends)
    meta = jnp.where(lane == n_blk, run_sel, starts)
    real = lane < n_blk
    for j in range(cap // GATHER_ROWS):
        c0 = float(j * GATHER_ROWS)
        first = jnp.sum(jnp.where(real & (ends <= c0), 1.0, 0.0), axis=1, keepdims=True)
        stop = n_blk - jnp.sum(jnp.where(real & (starts >= c0 + GATHER_ROWS), 1.0, 0.0), axis=1, keepdims=True)
        meta = jnp.where(lane == META_LO + j, first, meta)
        meta = jnp.where(lane == META_HI + j, stop, meta)
    meta_ref[0] = meta.astype(jnp.int32)


def _route(aff, cap):
    bsz, n_e, seq = aff.shape
    return pl.pallas_call(
        functools.partial(_route_body, cap=cap),
        grid=(bsz,),
        in_specs=[pl.BlockSpec((1, n_e, seq), lambda b: (b, 0, 0))],
        out_specs=[pl.BlockSpec((1, n_e, seq), lambda b: (b, 0, 0)),
                   pl.BlockSpec((1, seq, LANES), lambda b: (b, 0, 0)),
                   pl.BlockSpec((1, seq, LANES), lambda b: (b, 0, 0)),
                   pl.BlockSpec((1, n_e, LANES), lambda b: (b, 0, 0))],
        out_shape=[jax.ShapeDtypeStruct((bsz, n_e, seq), F32),
                   jax.ShapeDtypeStruct((bsz, seq, LANES), F32),
                   jax.ShapeDtypeStruct((bsz, seq, LANES), F32),
                   jax.ShapeDtypeStruct((bsz, n_e, LANES), jnp.int32)],
        compiler_params=pltpu.CompilerParams(
            dimension_semantics=("parallel",), vmem_limit_bytes=VMEM_LIMIT),
        name="route",
    )(aff)


GATHER_SPAN = 6


def _gather_body(meta_ref, h_ref, pos_ref, xin_ref, *, cap):
    b, e = pl.program_id(0), pl.program_id(1)
    base = (b * N_EXPERTS + e) * LANES
    n_blk = pos_ref.shape[2]
    half = GATHER_SPAN // 2
    slot = lax.broadcasted_iota(jnp.int32, (GATHER_ROWS, 1), 0)

    def picked(want, blk0, begin):
        pieces = []
        for t in range(half):
            pos = pos_ref[0, 0, pl.ds(blk0 + t, 1), :]
            if begin is not None:
                pos = jnp.where(blk0 + t >= begin, pos, -1.0)
            pieces.append(jnp.where(pos == want, 1.0, 0.0))
        onehot = jnp.concatenate(pieces, axis=1).astype(BF16)
        hblk = h_ref[0, pl.ds(pl.multiple_of(blk0 * TOK_BLK, TOK_BLK), half * TOK_BLK), :]
        return _dot(onehot, hblk)

    tiles = range(cap // GATHER_ROWS)
    for j in tiles:
        rows = slice(j * GATHER_ROWS, (j + 1) * GATHER_ROWS)
        want = (slot + j * GATHER_ROWS).astype(F32)
        blk0 = jnp.minimum(meta_ref[base + META_LO + j], n_blk - GATHER_SPAN)
        xin_ref[0, 0, rows, :] = (picked(want, blk0, None) + picked(want, blk0 + half, None)).astype(xin_ref.dtype)
    for j in tiles:
        rows = slice(j * GATHER_ROWS, (j + 1) * GATHER_ROWS)
        first = meta_ref[base + META_LO + j]
        stop = meta_ref[base + META_HI + j]

        @pl.when(stop > first + GATHER_SPAN)
        def _(j=j, rows=rows, first=first, stop=stop):
            want = (slot + j * GATHER_ROWS).astype(F32)

            def more(i, carry):
                begin = first + GATHER_SPAN + half * i
                extra = picked(want, jnp.minimum(begin, n_blk - half), begin)
                xin_ref[0, 0, rows, :] = (xin_ref[0, 0, rows, :].astype(F32) + extra).astype(xin_ref.dtype)
                return carry

            lax.fori_loop(0, (stop - first - GATHER_SPAN + half - 1) // half, more, 0)


def _gather(meta_flat, h2, pos, cap):
    bsz, seq, d_model = h2.shape
    n_e = pos.shape[1]
    assert seq // TOK_BLK >= GATHER_SPAN
    pos4 = pos.reshape(bsz, n_e, seq // TOK_BLK, TOK_BLK)
    return pl.pallas_call(
        functools.partial(_gather_body, cap=cap),
        grid_spec=pltpu.PrefetchScalarGridSpec(
            num_scalar_prefetch=1,
            grid=(bsz, n_e),
            in_specs=[pl.BlockSpec((1, seq, d_model), lambda b, e, m: (b, 0, 0)),
                      pl.BlockSpec((1, 1, seq // TOK_BLK, TOK_BLK), lambda b, e, m: (b, e, 0, 0))],
            out_specs=pl.BlockSpec((1, 1, cap, d_model), lambda b, e, m: (b, e, 0, 0))),
        out_shape=jax.ShapeDtypeStruct((bsz, n_e, cap, d_model), BF16),
        compiler_params=pltpu.CompilerParams(
            dimension_semantics=("parallel", "parallel"), vmem_limit_bytes=VMEM_LIMIT),
        name="gather",
    )(meta_flat, h2, pos4)


FFN_ROWS = 256


def _ffn_body(x_ref, wg_ref, wu_ref, wd_ref, y_ref):
    cap = x_ref.shape[2]
    for r in range(cap // FFN_ROWS):
        rows = slice(r * FFN_ROWS, (r + 1) * FFN_ROWS)
        xin = x_ref[0, 0, rows, :]
        a = _dot(xin, wg_ref[0])
        u = _dot(xin, wu_ref[0])
        mid = (a * jax.nn.sigmoid(a) * u).astype(BF16)
        y_ref[0, 0, rows, :] = _dot(mid, wd_ref[0]).astype(y_ref.dtype)


def _ffn(xin, w_gate, w_up, w_down):
    bsz, n_e, cap, d_model = xin.shape
    ff = w_gate.shape[2]
    tok = pl.BlockSpec((1, 1, cap, d_model), lambda e, b: (b, e, 0, 0))
    return pl.pallas_call(
        _ffn_body,
        grid=(n_e, bsz),
        in_specs=[tok,
                  pl.BlockSpec((1, d_model, ff), lambda e, b: (e, 0, 0)),
                  pl.BlockSpec((1, d_model, ff), lambda e, b: (e, 0, 0)),
                  pl.BlockSpec((1, ff, d_model), lambda e, b: (e, 0, 0))],
        out_specs=tok,
        out_shape=jax.ShapeDtypeStruct((bsz, n_e, cap, d_model), BF16),
        compiler_params=pltpu.CompilerParams(
            dimension_semantics=("parallel", "parallel"), vmem_limit_bytes=VMEM_LIMIT),
        name="ffn",
    )(xin, w_gate.astype(BF16), w_up.astype(BF16), w_down.astype(BF16))


SCATTER_WIN = 128
SCATTER_ALIGN = 64
SCATTER_EXPERTS = 4


def _scatter_body(meta_ref, y_ref, post_ref, gt_ref, x2_ref, gf_ref, o_ref, *, tc):
    b, t, eg = pl.program_id(0), pl.program_id(1), pl.program_id(2)
    cap = y_ref.shape[2]
    n_blk = tc // TOK_BLK

    @pl.when(eg == 0)
    def _():
        o_ref[0] = x2_ref[0]

    lane = lax.broadcasted_iota(jnp.int32, (1, LANES), 1)
    slot = lax.broadcasted_iota(jnp.int32, (1, SCATTER_WIN), 1)

    def expert_range(kk, el):
        base = (b * N_EXPERTS + eg * SCATTER_EXPERTS + el) * LANES + t * n_blk + kk
        first = meta_ref[base]
        last = meta_ref[base + 1]
        start = jnp.minimum((first // SCATTER_ALIGN) * SCATTER_ALIGN, cap - SCATTER_WIN)
        return start, last

    def columns(kk, el):
        rows = slice(kk * TOK_BLK, (kk + 1) * TOK_BLK)
        mine = lane == eg * SCATTER_EXPERTS + el
        pcol = jnp.sum(jnp.where(mine, post_ref[0, rows, :], 0.0), axis=1, keepdims=True)
        gcol = jnp.sum(jnp.where(mine, gt_ref[0, rows, :], 0.0), axis=1, keepdims=True)
        return pcol, gcol

    def window(el, start):
        return y_ref[0, el, pl.ds(pl.multiple_of(start, SCATTER_ALIGN), SCATTER_WIN), :]

    for kk in range(n_blk):
        rows = slice(kk * TOK_BLK, (kk + 1) * TOK_BLK)
        for pair in range(SCATTER_EXPERTS // 2):
            sel, ywin = [], []
            for el in (2 * pair, 2 * pair + 1):
                start, _ = expert_range(kk, el)
                pcol, gcol = columns(kk, el)
                sel.append(jnp.where(pcol == (slot + start).astype(F32), gcol, 0.0))
                ywin.append(window(el, start))
            o_ref[0, rows, :] += _dot(jnp.concatenate(sel, axis=1).astype(BF16), jnp.concatenate(ywin, axis=0))

    for kk in range(n_blk):
        rows = slice(kk * TOK_BLK, (kk + 1) * TOK_BLK)
        for el in range(SCATTER_EXPERTS):
            start, last = expert_range(kk, el)

            @pl.when(last > start + SCATTER_WIN)
            def _(kk=kk, el=el, rows=rows, start=start, last=last):
                pcol, gcol = columns(kk, el)

                def more(j, carry):
                    begin = start + (j + 1) * SCATTER_WIN
                    begin_c = jnp.minimum(begin, cap - SCATTER_WIN)
                    want = slot + begin_c
                    sel = jnp.where((pcol == want.astype(F32)) & (want >= begin), gcol, 0.0)
                    o_ref[0, rows, :] += _dot(sel.astype(BF16), window(el, begin_c))
                    return carry

                lax.fori_loop(0, (last - start - 1) // SCATTER_WIN, more, 0)

    @pl.when(eg == pl.num_programs(2) - 1)
    def _():
        v = o_ref[0]
        ms = jnp.mean(v * v, axis=-1, keepdims=True)
        o_ref[0] = v * lax.rsqrt(ms + NORM_EPS) * gf_ref[...]


def _scatter(meta_flat, y, post, gt, x2, normf_g, tc):
    bsz, n_e, cap, d_model = y.shape
    seq = x2.shape[1]
    tc = min(tc, seq)
    gf = normf_g[None, :]
    assert n_e % SCATTER_EXPERTS == 0 and cap % SCATTER_ALIGN == 0 and cap >= SCATTER_WIN
    tokw = lambda w, **kw: pl.BlockSpec((1, tc, w), lambda b, t, e, m: (b, t, 0), **kw)
    return pl.pallas_call(
        functools.partial(_scatter_body, tc=tc),
        grid_spec=pltpu.PrefetchScalarGridSpec(
            num_scalar_prefetch=1,
            grid=(bsz, seq // tc, n_e // SCATTER_EXPERTS),
            in_specs=[pl.BlockSpec((1, SCATTER_EXPERTS, cap, d_model), lambda b, t, e, m: (b, e, 0, 0)),
                      tokw(LANES), tokw(LANES),
                      tokw(d_model, pipeline_mode=pl.Buffered(1)),
                      pl.BlockSpec(gf.shape, lambda b, t, e, m: (0, 0))],
            out_specs=tokw(d_model)),
        out_shape=jax.ShapeDtypeStruct((bsz, seq, d_model), F32),
        compiler_params=pltpu.CompilerParams(
            dimension_semantics=("parallel", "parallel", "arbitrary"), vmem_limit_bytes=VMEM_LIMIT),
        name="scatter",
    )(meta_flat, y, post, gt, x2, gf)


class _Tiles(NamedTuple):
    rows: int
    attn_q: int
    attn_k: int
    band_q: int
    scatter_tokens: int


def _tiles(seq):
    return _Tiles(rows=min(512, seq), attn_q=min(128, seq), attn_k=min(1024, seq),
                  band_q=512, scatter_tokens=min(2048, seq))


def kernel(x, norm1_g, w_in, b_gates, q_norm_g, k_norm_g, w_o_a, w_o_b, w_out, norm2_g, w_router,
           w_gate, w_up, w_down, normf_g):
    bsz, seq, d_model = x.shape
    assert norm1_g.shape[0] == 1
    cap = EC_CAPACITY_FACTOR * seq // N_EXPERTS
    t = _tiles(seq)
    qa, ka, va, gate_a, gate_b, qkv_b = _in_proj(
        x, norm1_g[0], w_in[0], b_gates[0], q_norm_g[0], k_norm_g[0], t.rows)
    oa = _attn_a(qa, ka, va, tq=t.attn_q, tk=t.attn_k)
    obs, lses = [], []
    for gi, (_, dil) in enumerate(B_GROUPS):
        o, lse = _attn_b_group(*qkv_b[gi], gi, dil, tu=t.band_q)
        obs.append(o)
        lses.append(lse)
    x2, h2, aff = _merge(x, oa, obs, lses, gate_a, gate_b, w_o_a[0], w_o_b[0], w_out[0],
                         norm2_g[0], w_router[0], t.rows)
    pos, post, gt, meta = _route(aff, cap)
    meta_flat = meta.reshape(-1)
    xin = _gather(meta_flat, h2, pos, cap)
    y = _ffn(xin, w_gate[0], w_up[0], w_down[0])
    return _scatter(meta_flat, y, post, gt, x2, normf_g, tc=t.scatter_tokens)
```
